```python
import jax, jax.numpy as jnp
from jax import lax
import numpy as np

D_MODEL = 1024
BATCH = 8
SEQ = 8192
DEPTH = 1
DEC_BATCH = 8
DEC_SEQ = 32
PAST_LEN = 1024

CHUNK = 64
Q_BLOCK = 128
MIX_WIDTH = D_MODEL
A_WIDTH = MIX_WIDTH // 2
B_WIDTH = MIX_WIDTH - A_WIDTH
A_HEADS = 4
A_DK = A_WIDTH // A_HEADS
A_DV = A_WIDTH // A_HEADS
B_HEADS = 8
B_DH = B_WIDTH // B_HEADS
D_FF = 4 * D_MODEL
EPS = 1e-6
SPLITS = [A_WIDTH, 2 * A_WIDTH, 3 * A_WIDTH, 4 * A_WIDTH,
          4 * A_WIDTH + B_WIDTH, 4 * A_WIDTH + 2 * B_WIDTH, 4 * A_WIDTH + 3 * B_WIDTH]
IN_COLS = 4 * A_WIDTH + 3 * B_WIDTH + B_HEADS

kernel_name = 'hymba_hgrn2_fox_streaming_step'


def rms_norm(x, g):
    xf = x.astype(jnp.float32)
    y = xf * lax.rsqrt(jnp.mean(xf * xf, axis=-1, keepdims=True) + EPS)
    return (y * g.astype(jnp.float32)).astype(x.dtype)


def mixer_inputs(x, norm1, w_in, b_fox_f, q_gain, k_gain, lb):
    B, T, _ = x.shape
    h = rms_norm(x, norm1)
    z = h @ w_in
    qa, fa, ia, ga, qb, kb, vb, fb = jnp.split(z, SPLITS, axis=-1)
    f = lb + (1.0 - lb) * jax.nn.sigmoid(fa.astype(jnp.float32))
    hq = qa.astype(jnp.float32).reshape(B, T, A_HEADS, A_DK)
    hk = (1.0 - f).reshape(B, T, A_HEADS, A_DK)
    hlogf = jnp.log(f).reshape(B, T, A_HEADS, A_DK)
    hv = ia.astype(jnp.float32).reshape(B, T, A_HEADS, A_DV)
    fq = rms_norm(qb.reshape(B, T, B_HEADS, B_DH), q_gain)
    fk = rms_norm(kb.reshape(B, T, B_HEADS, B_DH), k_gain)
    fv = vb.reshape(B, T, B_HEADS, B_DH)
    flogf = jax.nn.log_sigmoid(fb.astype(jnp.float32) + b_fox_f.astype(jnp.float32))
    return (hq, hk, hv, hlogf, ga), (fq, fk, fv, flogf)


def hgrn2_chunk(S0, q, k, v, logf):
    C = q.shape[1]
    b = jnp.cumsum(logf, axis=1)
    b_ref = b[:, C // 2:C // 2 + 1]
    inter = jnp.einsum('bchk,bhkv->bchv', q * jnp.exp(b), S0)
    qr = q * jnp.exp(b - b_ref)
    kr = k * jnp.exp(b_ref - b)
    A = jnp.einsum('bchk,bshk->bhcs', qr, kr)
    mask = jnp.tril(jnp.ones((C, C), dtype=bool))
    A = jnp.where(mask[None, None], A, 0.0)
    o = inter + jnp.einsum('bhcs,bshv->bchv', A, v)
    b_last = b[:, -1:]
    S_new = jnp.exp(b_last[:, 0])[..., None] * S0 + jnp.einsum('bshk,bshv->bhkv', k * jnp.exp(b_last - b), v)
    return S_new, o


def hgrn2_prompt(q, k, v, logf):
    B, T, H, _ = q.shape
    nc = T // CHUNK

    def to_chunks(t):
        return jnp.moveaxis(t.reshape(B, nc, CHUNK, H, t.shape[-1]), 1, 0)

    def step(S_c, xs):
        return hgrn2_chunk(S_c, *xs)

    S0 = jnp.zeros((B, H, A_DK, A_DV), jnp.float32)
    S_fin, o = lax.scan(step, S0, (to_chunks(q), to_chunks(k), to_chunks(v), to_chunks(logf)))
    return S_fin, jnp.moveaxis(o, 0, 1).reshape(B, T, H, A_DV)


def fox_prompt_attention(q, k, v, logf):
    B, T, H, Dh = q.shape
    nb = T // Q_BLOCK
    c = jnp.cumsum(logf, axis=1)
    cT = jnp.transpose(c, (0, 2, 1))
    pos = jnp.arange(T)
    qblk = jnp.moveaxis(q.reshape(B, nb, Q_BLOCK, H, Dh), 1, 0)
    cblk = jnp.moveaxis(c.reshape(B, nb, Q_BLOCK, H), 1, 0)
    pblk = pos.reshape(nb, Q_BLOCK)
    scale = Dh ** -0.5

    def block(args):
        qi, ci, pi = args
        s = jnp.einsum('bqhd,bkhd->bhqk', qi, k, preferred_element_type=jnp.float32) * scale
        s = s + jnp.transpose(ci, (0, 2, 1))[..., :, None] - cT[..., None, :]
        s = jnp.where(pi[:, None] >= pos[None, :], s, -jnp.inf)
        p = jax.nn.softmax(s, axis=-1)
        return jnp.einsum('bhqk,bkhd->bqhd', p.astype(v.dtype), v)

    o = lax.map(block, (qblk, cblk, pblk))
    return jnp.moveaxis(o, 0, 1).reshape(B, T, H * Dh)


def fox_sample_attention(q, k_new, v_new, logf_new, k_cache, v_cache, logf_cache):
    B, T, H, Dh = q.shape
    P = k_cache.shape[1]
    k = jnp.concatenate([k_cache.astype(k_new.dtype), k_new], axis=1)
    v = jnp.concatenate([v_cache.astype(v_new.dtype), v_new], axis=1)
    c = jnp.cumsum(jnp.concatenate([logf_cache.astype(jnp.float32), logf_new], axis=1), axis=1)
    cT = jnp.transpose(c, (0, 2, 1))
    s = jnp.einsum('bqhd,bkhd->bhqk', q, k, preferred_element_type=jnp.float32) * (Dh ** -0.5)
    s = s + cT[..., P:, None] - cT[..., None, :]
    qpos = P + jnp.arange(T)
    kpos = jnp.arange(P + T)
    s = jnp.where(qpos[:, None] >= kpos[None, :], s, -jnp.inf)
    p = jax.nn.softmax(s, axis=-1)
    o = jnp.einsum('bhqk,bkhd->bqhd', p.astype(v.dtype), v)
    return o.reshape(B, T, H * Dh)


def mixer_output_and_ffn(x, o_a, ga, o_b, hgrn_out_norm, w_out, norm2, w_up, w_down):
    B, T, _ = x.shape
    oa = rms_norm(o_a, hgrn_out_norm.reshape(A_HEADS, A_DV))
    oa = (oa.reshape(B, T, A_WIDTH) * jax.nn.silu(ga.astype(jnp.float32))).astype(x.dtype)
    mix = jnp.concatenate([oa, o_b.astype(x.dtype)], axis=-1)
    x = x + mix @ w_out
    h = rms_norm(x, norm2)
    return x + jnp.square(jax.nn.relu(h @ w_up)) @ w_down


def setup_inputs(seed: int = 0) -> dict:
    key = jax.random.key(seed)
    ks = jax.random.split(key, 20)
    nrm = jax.random.normal
    f32 = jnp.float32
    return {
        'x_prompt': nrm(ks[0], (BATCH, SEQ, D_MODEL), f32),
        'x_sample': nrm(ks[1], (DEC_BATCH, DEC_SEQ, D_MODEL), f32),
        'cache_fox_k': nrm(ks[2], (DEPTH, DEC_BATCH, PAST_LEN, B_HEADS, B_DH), f32),
        'cache_fox_v': nrm(ks[3], (DEPTH, DEC_BATCH, PAST_LEN, B_HEADS, B_DH), f32),
        'cache_fox_logf': jax.nn.log_sigmoid(2.0 + nrm(ks[4], (DEPTH, DEC_BATCH, PAST_LEN, B_HEADS), f32)),
        'state_hgrn': 0.5 * nrm(ks[5], (DEPTH, DEC_BATCH, A_HEADS, A_DK, A_DV), f32),
        'norm1': 1.0 + 0.05 * nrm(ks[6], (DEPTH, D_MODEL), f32),
        'w_in': nrm(ks[7], (DEPTH, D_MODEL, IN_COLS), f32) * D_MODEL ** -0.5,
        'b_fox_f': 2.0 + 0.1 * nrm(ks[8], (DEPTH, B_HEADS), f32),
        'q_norm_gain': 1.0 + 0.05 * nrm(ks[9], (DEPTH, B_DH), f32),
        'k_norm_gain': 1.0 + 0.05 * nrm(ks[10], (DEPTH, B_DH), f32),
        'hgrn_lb_logits': 0.1 * nrm(ks[11], (DEPTH + 1, A_WIDTH), f32),
        'hgrn_out_norm': 1.0 + 0.05 * nrm(ks[12], (DEPTH, A_WIDTH), f32),
        'w_out': nrm(ks[13], (DEPTH, MIX_WIDTH, D_MODEL), f32) * MIX_WIDTH ** -0.5,
        'norm2': 1.0 + 0.05 * nrm(ks[14], (DEPTH, D_MODEL), f32),
        'w_up': nrm(ks[15], (DEPTH, D_MODEL, D_FF), f32) * D_MODEL ** -0.5,
        'w_down': nrm(ks[16], (DEPTH, D_FF, D_MODEL), f32) * D_FF ** -0.5,
    }


def reference(x_prompt, x_sample, cache_fox_k, cache_fox_v, cache_fox_logf, state_hgrn,
              norm1, w_in, b_fox_f, q_norm_gain, k_norm_gain, hgrn_lb_logits, hgrn_out_norm,
              w_out, norm2, w_up, w_down):
    lb_all = jnp.cumsum(jax.nn.softmax(hgrn_lb_logits.astype(jnp.float32), axis=0), axis=0)
    xp, xs = x_prompt, x_sample
    kp, vp, lp, sp = [], [], [], []
    ksl, vsl, lsl, ssl = [], [], [], []
    for l in range(DEPTH):
        lb = lb_all[l]
        (hq, hk, hv, hlf, ga), (fq, fk, fv, flf) = mixer_inputs(
            xp, norm1[l], w_in[l], b_fox_f[l], q_norm_gain[l], k_norm_gain[l], lb)
        S_p, o_a = hgrn2_prompt(hq, hk, hv, hlf)
        o_b = fox_prompt_attention(fq, fk, fv, flf)
        xp = mixer_output_and_ffn(xp, o_a, ga, o_b, hgrn_out_norm[l], w_out[l], norm2[l], w_up[l], w_down[l])
        kp.append(fk); vp.append(fv); lp.append(flf); sp.append(S_p.astype(x_prompt.dtype))
        (hq, hk, hv, hlf, ga), (fq, fk, fv, flf) = mixer_inputs(
            xs, norm1[l], w_in[l], b_fox_f[l], q_norm_gain[l], k_norm_gain[l], lb)
        S_s, o_a = hgrn2_chunk(state_hgrn[l].astype(jnp.float32), hq, hk, hv, hlf)
        o_b = fox_sample_attention(fq, fk, fv, flf, cache_fox_k[l], cache_fox_v[l], cache_fox_logf[l])
        xs = mixer_output_and_ffn(xs, o_a, ga, o_b, hgrn_out_norm[l], w_out[l], norm2[l], w_up[l], w_down[l])
        ksl.append(fk); vsl.append(fv); lsl.append(flf); ssl.append(S_s.astype(state_hgrn.dtype))
    return (xp, xs, jnp.stack(kp), jnp.stack(vp), jnp.stack(lp), jnp.stack(sp),
            jnp.stack(ksl), jnp.stack(vsl), jnp.stack(lsl), jnp.stack(ssl))
```

```python
import functools

import jax
import jax.numpy as jnp
from jax import lax
from jax.experimental import pallas as pl
from jax.experimental.pallas import tpu as pltpu

F32 = jnp.float32
BF16 = jnp.bfloat16

EPS = 1e-6
HGRN_CHUNK = 64
LANES = 128
NEG_BIG = -1e30
VMEM_LIMIT = 56 * 1024 * 1024


def _const_spec(shape):
    nd = len(shape)
    return pl.BlockSpec(shape, lambda *_: (0,) * nd, pipeline_mode=pl.Buffered(1))


def _cumsum_rows(x):
    n = x.shape[0]
    row = lax.broadcasted_iota(jnp.int32, x.shape, 0)
    s = 1
    while s < n:
        x = x + jnp.where(row >= s, pltpu.roll(x, s, axis=0), 0.0)
        s *= 2
    return x


def _split3(c):
    hi = c.astype(BF16).astype(F32)
    r1 = c - hi
    mid = r1.astype(BF16).astype(F32)
    lo = r1 - mid
    return hi, mid, lo


def _head_lanes(pair, odd):
    return pltpu.roll(pair, 64, axis=1) if odd else pair


def _pack_q(pair, c, h, dh):
    lane = lax.broadcasted_iota(jnp.int32, pair.shape, 1)
    hi, mid, lo = _split3(jnp.broadcast_to(c[:, h:h + 1], pair.shape))
    ext = jnp.where(lane == dh, hi,
          jnp.where(lane == dh + 1, mid,
          jnp.where(lane == dh + 2, lo,
          jnp.where(lane < dh + 6, 1.0, 0.0))))
    return jnp.where(lane < dh, _head_lanes(pair, h % 2), ext).astype(BF16)


def _pack_k(pair, c, h, dh):
    lane = lax.broadcasted_iota(jnp.int32, pair.shape, 1)
    hi, mid, lo = _split3(jnp.broadcast_to(c[:, h:h + 1], pair.shape))
    ext = jnp.where(lane == dh + 3, -hi,
          jnp.where(lane == dh + 4, -mid,
          jnp.where(lane == dh + 5, -lo,
          jnp.where(lane < dh + 3, 1.0, 0.0))))
    return jnp.where(lane < dh, _head_lanes(pair, h % 2), ext).astype(BF16)


def _pack_v(pair, h, dh):
    lane = lax.broadcasted_iota(jnp.int32, pair.shape, 1)
    ext = jnp.where(lane == dh, 1.0, 0.0)
    return jnp.where(lane < dh, _head_lanes(pair, h % 2), ext).astype(BF16)


def _pair_rms_norm(z, gain2, dh):
    lane = lax.broadcasted_iota(jnp.int32, z.shape, 1)
    first = lane < dh
    sq = z * z
    s0 = jnp.sum(jnp.where(first, sq, 0.0), axis=-1, keepdims=True)
    s1 = jnp.sum(jnp.where(first, 0.0, sq), axis=-1, keepdims=True)
    ms = jnp.where(first, s0, s1) * (1.0 / dh)
    return z * lax.rsqrt(ms + EPS) * gain2


def _log_sigmoid(x):
    return jnp.minimum(x, 0.0) - jnp.log1p(jnp.exp(-jnp.abs(x)))


def _proj_kernel(x_ref, n1_ref, w_ref, bf_ref, qg_ref, kg_ref, lbl_ref, cin_ref,
                 k_out, v_out, lf_out, qaug_out, kaug_out, vaug_out,
                 hq_out, hk_out, hv_out, hlf_out, ga_out, ctot_out,
                 z_scr, carry_scr, *, layer, aw, bw, nh, dh, rb):
    i = pl.program_id(1)

    @pl.when(i == 0)
    def _():
        carry_scr[...] = cin_ref[0]

    x = x_ref[0]
    ms = jnp.mean(x * x, axis=-1, keepdims=True)
    h = (x * lax.rsqrt(ms + EPS) * n1_ref[...]).astype(BF16)
    z_scr[...] = jnp.dot(h, w_ref[...], preferred_element_type=F32)

    lg = lbl_ref[...]
    e = jnp.exp(lg - jnp.max(lg, axis=0, keepdims=True))
    lb = jnp.sum(e[:layer + 1], axis=0, keepdims=True) / jnp.sum(e, axis=0, keepdims=True)

    q0 = 4 * aw
    k0 = q0 + bw
    v0 = k0 + bw
    f0 = v0 + bw
    scale = dh ** -0.5
    ta = x.shape[0]

    def body(r, carry):
        rows = pl.ds(pl.multiple_of(r * rb, rb), rb)
        hq_out[0, rows, :] = z_scr[rows, 0:aw].astype(BF16)
        f = lb + (1.0 - lb) * jax.nn.sigmoid(z_scr[rows, aw:2 * aw])
        hk_out[0, rows, :] = (1.0 - f).astype(BF16)
        hlf_out[0, rows, :] = jnp.log(f)
        hv_out[0, rows, :] = z_scr[rows, 2 * aw:3 * aw].astype(BF16)
        ga_out[0, rows, :] = z_scr[rows, 3 * aw:4 * aw].astype(BF16)

        lf = _log_sigmoid(z_scr[rows, f0:f0 + LANES] + bf_ref[...])
        lf_out[0, rows, :] = lf[:, :nh]
        c = _cumsum_rows(lf) + carry
        for p in range(nh // 2):
            cols = slice(LANES * p, LANES * (p + 1))
            qn = _pair_rms_norm(z_scr[rows, q0 + LANES * p:q0 + LANES * (p + 1)], qg_ref[...], dh) * scale
            kn = _pair_rms_norm(z_scr[rows, k0 + LANES * p:k0 + LANES * (p + 1)], kg_ref[...], dh)
            vv = z_scr[rows, v0 + LANES * p:v0 + LANES * (p + 1)]
            k_out[0, rows, cols] = kn
            v_out[0, rows, cols] = vv
            for odd in range(2):
                hd = 2 * p + odd
                qaug_out[0, hd, rows, :] = _pack_q(qn, c, hd, dh)
                kaug_out[0, hd, rows, :] = _pack_k(kn, c, hd, dh)
                vaug_out[0, hd, rows, :] = _pack_v(vv, hd, dh)
        return c[rb - 1:rb, :]

    carry = lax.fori_loop(0, ta // rb, body, carry_scr[...])
    carry_scr[...] = carry
    ctot_out[0] = carry


def _proj(x, n1, w_pad, bf_pad, qg2, kg2, lb_logits, carry_in, *, layer, aw, bw, nh, dh):
    b, t, d = x.shape
    ta = min(512, t)
    rb = min(64, ta)
    ncol = w_pad.shape[1]
    tok = lambda bi, i: (bi, i, 0)
    headed = lambda bi, i: (bi, 0, i, 0)
    per_b = lambda bi, i: (bi, 0, 0)
    out_shape = (
        jax.ShapeDtypeStruct((b, t, bw), F32),
        jax.ShapeDtypeStruct((b, t, bw), F32),
        jax.ShapeDtypeStruct((b, t, nh), F32),
        jax.ShapeDtypeStruct((b, nh, t, LANES), BF16),
        jax.ShapeDtypeStruct((b, nh, t, LANES), BF16),
        jax.ShapeDtypeStruct((b, nh, t, LANES), BF16),
        jax.ShapeDtypeStruct((b, t, aw), BF16),
        jax.ShapeDtypeStruct((b, t, aw), BF16),
        jax.ShapeDtypeStruct((b, t, aw), BF16),
        jax.ShapeDtypeStruct((b, t, aw), F32),
        jax.ShapeDtypeStruct((b, t, aw), BF16),
        jax.ShapeDtypeStruct((b, 1, LANES), F32),
    )
    out_specs = (
        pl.BlockSpec((1, ta, bw), tok), pl.BlockSpec((1, ta, bw), tok), pl.BlockSpec((1, ta, nh), tok),
        pl.BlockSpec((1, nh, ta, LANES), headed), pl.BlockSpec((1, nh, ta, LANES), headed),
        pl.BlockSpec((1, nh, ta, LANES), headed),
        pl.BlockSpec((1, ta, aw), tok), pl.BlockSpec((1, ta, aw), tok), pl.BlockSpec((1, ta, aw), tok),
        pl.BlockSpec((1, ta, aw), tok), pl.BlockSpec((1, ta, aw), tok),
        pl.BlockSpec((1, 1, LANES), per_b),
    )
    in_specs = [
        pl.BlockSpec((1, ta, d), tok),
        _const_spec(n1.shape), _const_spec(w_pad.shape), _const_spec(bf_pad.shape),
        _const_spec(qg2.shape), _const_spec(kg2.shape), _const_spec(lb_logits.shape),
        pl.BlockSpec((1, 1, LANES), per_b),
    ]
    return pl.pallas_call(
        functools.partial(_proj_kernel, layer=layer, aw=aw, bw=bw, nh=nh, dh=dh, rb=rb),
        out_shape=out_shape,
        grid=(b, t // ta),
        in_specs=in_specs,
        out_specs=out_specs,
        scratch_shapes=[pltpu.VMEM((ta, ncol), F32), pltpu.VMEM((1, LANES), F32)],
        compiler_params=pltpu.CompilerParams(
            dimension_semantics=("arbitrary", "arbitrary"), vmem_limit_bytes=VMEM_LIMIT),
        name="proj",
    )(x, n1, w_pad, bf_pad, qg2, kg2, lb_logits, carry_in)


def _cache_prep_kernel(k_ref, v_ref, lf_ref, kaug_out, vaug_out, ctot_out, *, nh, dh, rb):
    t = k_ref.shape[1]

    def body(r, carry):
        rows = pl.ds(pl.multiple_of(r * rb, rb), rb)
        c = _cumsum_rows(lf_ref[0, rows, :]) + carry
        for p in range(nh // 2):
            cols = slice(LANES * p, LANES * (p + 1))
            kk = k_ref[0, rows, cols]
            vv = v_ref[0, rows, cols]
            for odd in range(2):
                hd = 2 * p + odd
                kaug_out[0, hd, rows, :] = _pack_k(kk, c, hd, dh)
                vaug_out[0, hd, rows, :] = _pack_v(vv, hd, dh)
        return c[rb - 1:rb, :]

    ctot_out[0] = lax.fori_loop(0, t // rb, body, jnp.zeros((1, LANES), F32))


def _cache_prep(ck, cv, clf_pad, *, nh, dh):
    b, t, bw = ck.shape
    rb = min(64, t)
    whole = lambda bi: (bi, 0, 0)
    whole4 = lambda bi: (bi, 0, 0, 0)
    return pl.pallas_call(
        functools.partial(_cache_prep_kernel, nh=nh, dh=dh, rb=rb),
        out_shape=(jax.ShapeDtypeStruct((b, nh, t, LANES), BF16),
                   jax.ShapeDtypeStruct((b, nh, t, LANES), BF16),
                   jax.ShapeDtypeStruct((b, 1, LANES), F32)),
        grid=(b,),
        in_specs=[pl.BlockSpec((1, t, bw), whole), pl.BlockSpec((1, t, bw), whole),
                  pl.BlockSpec((1, t, LANES), whole)],
        out_specs=(pl.BlockSpec((1, nh, t, LANES), whole4), pl.BlockSpec((1, nh, t, LANES), whole4),
                   pl.BlockSpec((1, 1, LANES), whole)),
        compiler_params=pltpu.CompilerParams(dimension_semantics=("arbitrary",)),
        name="cache_prep",
    )(ck, cv, clf_pad)


def _hgrn_kernel(q_ref, k_ref, v_ref, lf_ref, ga_ref, s0_ref, gn_ref, oa_out, s_out, st_scr,
                 *, nha, chunk):
    i = pl.program_id(1)
    th = q_ref.shape[1]

    @pl.when(i == 0)
    def _():
        for hd in range(nha):
            st_scr[hd] = s0_ref[0, hd].T

    ci = lax.broadcasted_iota(jnp.int32, (chunk, chunk), 0)
    si = lax.broadcasted_iota(jnp.int32, (chunk, chunk), 1)
    causal = ci >= si
    nt = (((1,), (1,)), ((), ()))
    tn = (((0,), (0,)), ((), ()))

    def body(cidx, _):
        rows = pl.ds(pl.multiple_of(cidx * chunk, chunk), chunk)
        for hd in range(nha):
            cols = slice(LANES * hd, LANES * (hd + 1))
            b = _cumsum_rows(lf_ref[0, rows, cols])
            b_ref = b[chunk // 2:chunk // 2 + 1, :]
            b_last = b[chunk - 1:chunk, :]
            q = q_ref[0, rows, cols].astype(F32)
            k = k_ref[0, rows, cols].astype(F32)
            v = v_ref[0, rows, cols]
            st = st_scr[hd]
            inter = lax.dot_general((q * jnp.exp(b)).astype(BF16), st.astype(BF16), nt,
                                    preferred_element_type=F32)
            qr = (q * jnp.exp(b - b_ref)).astype(BF16)
            kr = (k * jnp.exp(b_ref - b)).astype(BF16)
            a = lax.dot_general(qr, kr, nt, preferred_element_type=F32)
            a = jnp.where(causal, a, 0.0).astype(BF16)
            o = inter + jnp.dot(a, v, preferred_element_type=F32)
            k3 = (k * jnp.exp(b_last - b)).astype(BF16)
            st_scr[hd] = st * jnp.exp(b_last) + lax.dot_general(v, k3, tn, preferred_element_type=F32)
            ms = jnp.mean(o * o, axis=-1, keepdims=True)
            g = ga_ref[0, rows, cols].astype(F32)
            on = o * lax.rsqrt(ms + EPS) * gn_ref[:, cols]
            oa_out[0, rows, cols] = (on * (g * jax.nn.sigmoid(g))).astype(BF16)
        return 0

    lax.fori_loop(0, th // chunk, body, 0)

    @pl.when(i == pl.num_programs(1) - 1)
    def _():
        for hd in range(nha):
            s_out[0, hd] = st_scr[hd].T


def _hgrn(hq, hk, hv, hlf, ga, s0, gnorm, *, nha, chunk):
    b, t, aw = hq.shape
    th = min(512, t)
    dk, dv = s0.shape[2], s0.shape[3]
    tok = lambda bi, i: (bi, i, 0)
    st = lambda bi, i: (bi, 0, 0, 0)
    return pl.pallas_call(
        functools.partial(_hgrn_kernel, nha=nha, chunk=chunk),
        out_shape=(jax.ShapeDtypeStruct((b, t, aw), BF16),
                   jax.ShapeDtypeStruct((b, nha, dk, dv), F32)),
        grid=(b, t // th),
        in_specs=[pl.BlockSpec((1, th, aw), tok)] * 5
                 + [pl.BlockSpec((1, nha, dk, dv), st), _const_spec(gnorm.shape)],
        out_specs=(pl.BlockSpec((1, th, aw), tok), pl.BlockSpec((1, nha, dk, dv), st)),
        scratch_shapes=[pltpu.VMEM((nha, dv, dk), F32)],
        compiler_params=pltpu.CompilerParams(dimension_semantics=("arbitrary", "arbitrary")),
        name="hgrn",
    )(hq, hk, hv, hlf, ga, s0, gnorm)


def _flash_step(q, k, v, m_ref, acc_ref, hh, mask):
    s = lax.dot_general(q, k, (((1,), (1,)), ((), ())), preferred_element_type=F32)
    if mask is not None:
        s = jnp.where(mask, s, NEG_BIG)
    m_prev = m_ref[hh]
    m_new = jnp.maximum(m_prev, jnp.max(s, axis=-1, keepdims=True))
    alpha = jnp.exp(m_prev - m_new)
    p = jnp.concatenate(
        [jnp.exp(s[:, LANES * c:LANES * (c + 1)] - m_new) for c in range(s.shape[1] // LANES)], axis=1)
    acc_ref[hh] = alpha * acc_ref[hh] + jnp.dot(p.astype(BF16), v, preferred_element_type=F32)
    m_ref[hh] = m_new


def _merge_pair(acc_ref, dh):
    outs = []
    for hh in range(2):
        acc = acc_ref[hh]
        outs.append(acc / jnp.broadcast_to(acc[:, dh:dh + 1], acc.shape))
    lane = lax.broadcasted_iota(jnp.int32, outs[0].shape, 1)
    return jnp.where(lane < dh, outs[0], pltpu.roll(outs[1], dh, axis=1))


def _attn_kernel(q_ref, k_ref, v_ref, o_ref, m_scr, acc_scr, *, tq, dh):
    i = pl.program_id(2)
    m_scr[...] = jnp.full(m_scr.shape, NEG_BIG, F32)
    acc_scr[...] = jnp.zeros(acc_scr.shape, F32)

    def body(j, _):
        rows = pl.ds(pl.multiple_of(j * tq, tq), tq)
        for hh in range(2):
            _flash_step(q_ref[0, hh], k_ref[0, hh, rows, :], v_ref[0, hh, rows, :], m_scr, acc_scr, hh, None)
        return 0

    lax.fori_loop(0, i, body, 0)

    ri = lax.broadcasted_iota(jnp.int32, (tq, tq), 0)
    ki = lax.broadcasted_iota(jnp.int32, (tq, tq), 1)
    diag = pl.ds(pl.multiple_of(i * tq, tq), tq)
    for hh in range(2):
        _flash_step(q_ref[0, hh], k_ref[0, hh, diag, :], v_ref[0, hh, diag, :], m_scr, acc_scr, hh, ri >= ki)
    o_ref[0] = _merge_pair(acc_scr, dh).astype(o_ref.dtype)


def _attn(qaug, kaug, vaug, *, dh):
    b, nh, t, _ = qaug.shape
    tq = min(512, t)
    return pl.pallas_call(
        functools.partial(_attn_kernel, tq=tq, dh=dh),
        out_shape=jax.ShapeDtypeStruct((b, t, nh * dh), BF16),
        grid=(b, nh // 2, t // tq),
        in_specs=[pl.BlockSpec((1, 2, tq, LANES), lambda bi, p, i: (bi, p, i, 0)),
                  pl.BlockSpec((1, 2, t, LANES), lambda bi, p, i: (bi, p, 0, 0)),
                  pl.BlockSpec((1, 2, t, LANES), lambda bi, p, i: (bi, p, 0, 0))],
        out_specs=pl.BlockSpec((1, tq, 2 * dh), lambda bi, p, i: (bi, i, p)),
        scratch_shapes=[pltpu.VMEM((2, tq, LANES), F32), pltpu.VMEM((2, tq, LANES), F32)],
        compiler_params=pltpu.CompilerParams(
            dimension_semantics=("arbitrary", "arbitrary", "arbitrary"), vmem_limit_bytes=VMEM_LIMIT),
        name="fox_attn",
    )(qaug, kaug, vaug)


def _sample_attn_kernel(q_ref, kc_ref, vc_ref, kn_ref, vn_ref, o_ref, acc_scr, *, nh, dh):
    s_len = q_ref.shape[2]
    ri = lax.broadcasted_iota(jnp.int32, (s_len, s_len), 0)
    ki = lax.broadcasted_iota(jnp.int32, (s_len, s_len), 1)
    nt = (((1,), (1,)), ((), ()))
    for p in range(nh // 2):
        for hh in range(2):
            hd = 2 * p + hh
            q = q_ref[0, hd]
            s1 = lax.dot_general(q, kc_ref[0, hd], nt, preferred_element_type=F32)
            s2 = lax.dot_general(q, kn_ref[0, hd], nt, preferred_element_type=F32)
            s2 = jnp.where(ri >= ki, s2, NEG_BIG)
            m = jnp.maximum(jnp.max(s1, axis=-1, keepdims=True), jnp.max(s2, axis=-1, keepdims=True))
            p1 = jnp.exp(s1 - m).astype(BF16)
            p2 = jnp.exp(s2 - m).astype(BF16)
            acc_scr[hh] = (jnp.dot(p1, vc_ref[0, hd], preferred_element_type=F32)
                           + jnp.dot(p2, vn_ref[0, hd], preferred_element_type=F32))
        o_ref[0, :, 2 * dh * p:2 * dh * (p + 1)] = _merge_pair(acc_scr, dh).astype(o_ref.dtype)


def _sample_attn(qaug, kaug_c, vaug_c, kaug_n, vaug_n, *, dh):
    b, nh, s_len, _ = qaug.shape
    p_len = kaug_c.shape[2]
    whole4 = lambda bi: (bi, 0, 0, 0)
    return pl.pallas_call(
        functools.partial(_sample_attn_kernel, nh=nh, dh=dh),
        out_shape=jax.ShapeDtypeStruct((b, s_len, nh * dh), BF16),
        grid=(b,),
        in_specs=[pl.BlockSpec((1, nh, s_len, LANES), whole4),
                  pl.BlockSpec((1, nh, p_len, LANES), whole4), pl.BlockSpec((1, nh, p_len, LANES), whole4),
                  pl.BlockSpec((1, nh, s_len, LANES), whole4), pl.BlockSpec((1, nh, s_len, LANES), whole4)],
        out_specs=pl.BlockSpec((1, s_len, nh * dh), lambda bi: (bi, 0, 0)),
        scratch_shapes=[pltpu.VMEM((2, s_len, LANES), F32)],
        compiler_params=pltpu.CompilerParams(dimension_semantics=("arbitrary",)),
        name="fox_sample_attn",
    )(qaug, kaug_c, vaug_c, kaug_n, vaug_n)


def _ffn_kernel(x_ref, oa_ref, ob_ref, woa_ref, wob_ref, n2_ref, wup_ref, wdn_ref, y_ref, h_scr, *, ff_chunk):
    x1 = (x_ref[...]
          + jnp.dot(oa_ref[...], woa_ref[...], preferred_element_type=F32)
          + jnp.dot(ob_ref[...], wob_ref[...], preferred_element_type=F32))
    ms = jnp.mean(x1 * x1, axis=-1, keepdims=True)
    h_scr[...] = (x1 * lax.rsqrt(ms + EPS) * n2_ref[...]).astype(BF16)
    y_ref[...] = x1
    for c in range(wup_ref.shape[1] // ff_chunk):
        cs = slice(ff_chunk * c, ff_chunk * (c + 1))
        u = jnp.maximum(jnp.dot(h_scr[...], wup_ref[:, cs], preferred_element_type=F32), 0.0)
        y_ref[...] += jnp.dot((u * u).astype(BF16), wdn_ref[cs, :], preferred_element_type=F32)


def _ffn(x2d, oa2d, ob2d, woa, wob, n2, wup, wdn):
    n, d = x2d.shape
    tc = min(512, n)
    aw, bw = oa2d.shape[1], ob2d.shape[1]
    row = lambda i: (i, 0)
    return pl.pallas_call(
        functools.partial(_ffn_kernel, ff_chunk=min(1024, wup.shape[1])),
        out_shape=jax.ShapeDtypeStruct((n, d), F32),
        grid=(n // tc,),
        in_specs=[pl.BlockSpec((tc, d), row), pl.BlockSpec((tc, aw), row), pl.BlockSpec((tc, bw), row),
                  _const_spec(woa.shape), _const_spec(wob.shape), _const_spec(n2.shape),
                  _const_spec(wup.shape), _const_spec(wdn.shape)],
        out_specs=pl.BlockSpec((tc, d), row),
        scratch_shapes=[pltpu.VMEM((tc, d), BF16)],
        compiler_params=pltpu.CompilerParams(
            dimension_semantics=("arbitrary",), vmem_limit_bytes=VMEM_LIMIT),
        name="out_ffn",
    )(x2d, oa2d, ob2d, woa, wob, n2, wup, wdn)


def kernel(x_prompt, x_sample, cache_fox_k, cache_fox_v, cache_fox_logf, state_hgrn, norm1, w_in, b_fox_f,
           q_norm_gain, k_norm_gain, hgrn_lb_logits, hgrn_out_norm, w_out, norm2, w_up, w_down):
    depth = w_in.shape[0]
    d = x_prompt.shape[-1]
    nh, dh = cache_fox_k.shape[3], cache_fox_k.shape[4]
    nha, dk, dv = state_hgrn.shape[2], state_hgrn.shape[3], state_hgrn.shape[4]
    aw, bw = nha * dk, nh * dh
    assert dk == LANES and dv == LANES and 2 * dh == LANES and nh % 2 == 0 and nh <= 8
    assert w_in.shape[2] == 4 * aw + 3 * bw + nh and aw + bw == w_out.shape[1]

    xp, xs = x_prompt, x_sample
    bp, tp, _ = xp.shape
    bs, ts, _ = xs.shape
    plen = cache_fox_k.shape[2]
    lb_logits = hgrn_lb_logits.astype(F32)

    kp, vp, lp, sp = [], [], [], []
    ksl, vsl, lsl, ssl = [], [], [], []
    for l in range(depth):
        w_pad = jnp.pad(w_in[l], ((0, 0), (0, LANES - nh))).astype(BF16)
        bf_pad = jnp.pad(b_fox_f[l].astype(F32), (0, LANES - nh)).reshape(1, LANES)
        qg2 = jnp.tile(q_norm_gain[l].astype(F32), 2).reshape(1, LANES)
        kg2 = jnp.tile(k_norm_gain[l].astype(F32), 2).reshape(1, LANES)
        n1 = norm1[l].astype(F32).reshape(1, d)
        n2 = norm2[l].astype(F32).reshape(1, d)
        gnorm = hgrn_out_norm[l].astype(F32).reshape(1, aw)
        woa = w_out[l, :aw].astype(BF16)
        wob = w_out[l, aw:].astype(BF16)
        wup = w_up[l].astype(BF16)
        wdn = w_down[l].astype(BF16)
        proj = functools.partial(_proj, layer=l, aw=aw, bw=bw, nh=nh, dh=dh)

        (k_p, v_p, lf_p, qaug, kaug, vaug, hq, hk, hv, hlf, ga, _) = proj(
            xp, n1, w_pad, bf_pad, qg2, kg2, lb_logits, jnp.zeros((bp, 1, LANES), F32))
        oa, s_p = _hgrn(hq, hk, hv, hlf, ga, jnp.zeros((bp, nha, dk, dv), F32), gnorm,
                        nha=nha, chunk=min(HGRN_CHUNK, tp))
        ob = _attn(qaug, kaug, vaug, dh=dh)
        xp = _ffn(xp.reshape(bp * tp, d), oa.reshape(bp * tp, aw), ob.reshape(bp * tp, bw),
                  woa, wob, n2, wup, wdn).reshape(bp, tp, d)
        kp.append(k_p.reshape(bp, tp, nh, dh))
        vp.append(v_p.reshape(bp, tp, nh, dh))
        lp.append(lf_p)
        sp.append(s_p.astype(x_prompt.dtype))

        clf_pad = jnp.pad(cache_fox_logf[l].astype(F32), ((0, 0), (0, 0), (0, LANES - nh)))
        kaug_c, vaug_c, c_tot = _cache_prep(
            cache_fox_k[l].astype(F32).reshape(bs, plen, bw), cache_fox_v[l].astype(F32).reshape(bs, plen, bw),
            clf_pad, nh=nh, dh=dh)
        (k_s, v_s, lf_s, qaug, kaug, vaug, hq, hk, hv, hlf, ga, _) = proj(
            xs, n1, w_pad, bf_pad, qg2, kg2, lb_logits, c_tot)
        oa, s_s = _hgrn(hq, hk, hv, hlf, ga, state_hgrn[l].astype(F32), gnorm, nha=nha, chunk=ts)
        ob = _sample_attn(qaug, kaug_c, vaug_c, kaug, vaug, dh=dh)
        xs = _ffn(xs.reshape(bs * ts, d), oa.reshape(bs * ts, aw), ob.reshape(bs * ts, bw),
                  woa, wob, n2, wup, wdn).reshape(bs, ts, d)
        ksl.append(k_s.reshape(bs, ts, nh, dh))
        vsl.append(v_s.reshape(bs, ts, nh, dh))
        lsl.append(lf_s)
        ssl.append(s_s.astype(state_hgrn.dtype))

    return (xp, xs, jnp.stack(kp), jnp.stack(vp), jnp.stack(lp), jnp.stack(sp),
            jnp.stack(ksl), jnp.stack(vsl), jnp.stack(lsl), jnp.stack(ssl))
```

```python
import functools

import jax
import jax.numpy as jnp
from jax import lax
from jax.experimental import pallas as pl
from jax.experimental.pallas import tpu as pltpu

F32 = jnp.float32
BF16 = jnp.bfloat16

EPS = 1e-6
HGRN_CHUNK = 64
ATTN_BLOCK = 512
LANES = 128
NEG_BIG = -1e30
LOG2E = 1.4426950408889634
EXP2_ZERO_BELOW = -150.0
VMEM_LIMIT = 56 * 1024 * 1024


def _const_spec(shape):
    nd = len(shape)
    return pl.BlockSpec(shape, lambda *_: (0,) * nd, pipeline_mode=pl.Buffered(1))


def _cumsum_rows(x):
    n = x.shape[0]
    row = lax.broadcasted_iota(jnp.int32, x.shape, 0)
    s = 1
    while s < n:
        x = x + jnp.where(row >= s, pltpu.roll(x, s, axis=0), 0.0)
        s *= 2
    return x


def _split3(c):
    hi = c.astype(BF16).astype(F32)
    r1 = c - hi
    mid = r1.astype(BF16).astype(F32)
    lo = r1 - mid
    return hi, mid, lo


def _head_lanes(pair, odd):
    return pltpu.roll(pair, 64, axis=1) if odd else pair


def _pack_q(pair, c, h, dh):
    lane = lax.broadcasted_iota(jnp.int32, pair.shape, 1)
    hi, mid, lo = _split3(jnp.broadcast_to(c[:, h:h + 1], pair.shape) * LOG2E)
    ext = jnp.where(lane == dh, hi,
          jnp.where(lane == dh + 1, mid,
          jnp.where(lane == dh + 2, lo,
          jnp.where(lane < dh + 6, 1.0, 0.0))))
    return jnp.where(lane < dh, _head_lanes(pair, h % 2), ext).astype(BF16)


def _pack_k(pair, c, h, dh):
    lane = lax.broadcasted_iota(jnp.int32, pair.shape, 1)
    hi, mid, lo = _split3(jnp.broadcast_to(c[:, h:h + 1], pair.shape) * LOG2E)
    ext = jnp.where(lane == dh + 3, -hi,
          jnp.where(lane == dh + 4, -mid,
          jnp.where(lane == dh + 5, -lo,
          jnp.where(lane < dh + 3, 1.0, 0.0))))
    return jnp.where(lane < dh, _head_lanes(pair, h % 2), ext).astype(BF16)


def _pack_v(pair, h, dh):
    lane = lax.broadcasted_iota(jnp.int32, pair.shape, 1)
    ext = jnp.where(lane == dh, 1.0, 0.0)
    return jnp.where(lane < dh, _head_lanes(pair, h % 2), ext).astype(BF16)


def _pair_rms_norm(z, gain2, dh):
    lane = lax.broadcasted_iota(jnp.int32, z.shape, 1)
    first = lane < dh
    sq = z * z
    s0 = jnp.sum(jnp.where(first, sq, 0.0), axis=-1, keepdims=True)
    s1 = jnp.sum(jnp.where(first, 0.0, sq), axis=-1, keepdims=True)
    ms = jnp.where(first, s0, s1) * (1.0 / dh)
    return z * lax.rsqrt(ms + EPS) * gain2


def _log_sigmoid(x):
    return jnp.minimum(x, 0.0) - jnp.log1p(jnp.exp(-jnp.abs(x)))


def _proj_kernel(x_ref, n1_ref, w_ref, bf_ref, qg_ref, kg_ref, lbl_ref, cin_ref,
                 k_out, v_out, lf_out, qaug_out, kaug_out, vaug_out,
                 hq_out, hk_out, hv_out, hlf_out, ga_out, ctot_out, cends_out,
                 z_scr, carry_scr, *, layer, aw, bw, nh, dh, rb):
    i = pl.program_id(1)

    @pl.when(i == 0)
    def _():
        carry_scr[...] = cin_ref[0]

    x = x_ref[0]
    ms = jnp.mean(x * x, axis=-1, keepdims=True)
    h = (x * lax.rsqrt(ms + EPS) * n1_ref[...]).astype(BF16)
    z_scr[...] = jnp.dot(h, w_ref[...], preferred_element_type=F32)

    lg = lbl_ref[...]
    e = jnp.exp(lg - jnp.max(lg, axis=0, keepdims=True))
    lb = jnp.sum(e[:layer + 1], axis=0, keepdims=True) / jnp.sum(e, axis=0, keepdims=True)

    q0 = 4 * aw
    k0 = q0 + bw
    v0 = k0 + bw
    f0 = v0 + bw
    scale = dh ** -0.5 * LOG2E
    ta = x.shape[0]
    nrb = ta // rb
    rb_idx = lax.broadcasted_iota(jnp.int32, (nrb, LANES), 0)

    def body(r, state):
        carry, cends = state
        rows = pl.ds(pl.multiple_of(r * rb, rb), rb)
        hq_out[0, rows, :] = z_scr[rows, 0:aw].astype(BF16)
        f = lb + (1.0 - lb) * jax.nn.sigmoid(z_scr[rows, aw:2 * aw])
        hk_out[0, rows, :] = (1.0 - f).astype(BF16)
        hlf_out[0, rows, :] = jnp.log(f)
        hv_out[0, rows, :] = z_scr[rows, 2 * aw:3 * aw].astype(BF16)
        ga_out[0, rows, :] = z_scr[rows, 3 * aw:4 * aw].astype(BF16)

        lf = _log_sigmoid(z_scr[rows, f0:f0 + LANES] + bf_ref[...])
        lf_out[0, rows, :] = lf[:, :nh]
        c = _cumsum_rows(lf) + carry
        for p in range(nh // 2):
            cols = slice(LANES * p, LANES * (p + 1))
            qn = _pair_rms_norm(z_scr[rows, q0 + LANES * p:q0 + LANES * (p + 1)], qg_ref[...], dh) * scale
            kn = _pair_rms_norm(z_scr[rows, k0 + LANES * p:k0 + LANES * (p + 1)], kg_ref[...], dh)
            vv = z_scr[rows, v0 + LANES * p:v0 + LANES * (p + 1)]
            k_out[0, rows, cols] = kn
            v_out[0, rows, cols] = vv
            for odd in range(2):
                hd = 2 * p + odd
                qaug_out[0, hd, rows, :] = _pack_q(qn, c, hd, dh)
                kaug_out[0, hd, rows, :] = _pack_k(kn, c, hd, dh)
                vaug_out[0, hd, rows, :] = _pack_v(vv, hd, dh)
        carry = c[rb - 1:rb, :]
        return carry, jnp.where(rb_idx == r, carry, cends)

    carry, cends = lax.fori_loop(0, nrb, body, (carry_scr[...], jnp.zeros((nrb, LANES), F32)))
    carry_scr[...] = carry
    ctot_out[0] = carry
    cends_out[0] = cends


def _proj(x, n1, w_pad, bf_pad, qg2, kg2, lb_logits, carry_in, *, layer, aw, bw, nh, dh):
    b, t, d = x.shape
    ta = min(512, t)
    rb = min(64, ta)
    ncol = w_pad.shape[1]
    tok = lambda bi, i: (bi, i, 0)
    headed = lambda bi, i: (bi, 0, i, 0)
    per_b = lambda bi, i: (bi, 0, 0)
    out_shape = (
        jax.ShapeDtypeStruct((b, t, bw), F32),
        jax.ShapeDtypeStruct((b, t, bw), F32),
        jax.ShapeDtypeStruct((b, t, nh), F32),
        jax.ShapeDtypeStruct((b, nh, t, LANES), BF16),
        jax.ShapeDtypeStruct((b, nh, t, LANES), BF16),
        jax.ShapeDtypeStruct((b, nh, t, LANES), BF16),
        jax.ShapeDtypeStruct((b, t, aw), BF16),
        jax.ShapeDtypeStruct((b, t, aw), BF16),
        jax.ShapeDtypeStruct((b, t, aw), BF16),
        jax.ShapeDtypeStruct((b, t, aw), F32),
        jax.ShapeDtypeStruct((b, t, aw), BF16),
        jax.ShapeDtypeStruct((b, 1, LANES), F32),
        jax.ShapeDtypeStruct((b, t // rb, LANES), F32),
    )
    out_specs = (
        pl.BlockSpec((1, ta, bw), tok), pl.BlockSpec((1, ta, bw), tok), pl.BlockSpec((1, ta, nh), tok),
        pl.BlockSpec((1, nh, ta, LANES), headed), pl.BlockSpec((1, nh, ta, LANES), headed),
        pl.BlockSpec((1, nh, ta, LANES), headed),
        pl.BlockSpec((1, ta, aw), tok), pl.BlockSpec((1, ta, aw), tok), pl.BlockSpec((1, ta, aw), tok),
        pl.BlockSpec((1, ta, aw), tok), pl.BlockSpec((1, ta, aw), tok),
        pl.BlockSpec((1, 1, LANES), per_b),
        pl.BlockSpec((1, ta // rb, LANES), tok),
    )
    in_specs = [
        pl.BlockSpec((1, ta, d), tok),
        _const_spec(n1.shape), _const_spec(w_pad.shape), _const_spec(bf_pad.shape),
        _const_spec(qg2.shape), _const_spec(kg2.shape), _const_spec(lb_logits.shape),
        pl.BlockSpec((1, 1, LANES), per_b),
    ]
    return pl.pallas_call(
        functools.partial(_proj_kernel, layer=layer, aw=aw, bw=bw, nh=nh, dh=dh, rb=rb),
        out_shape=out_shape,
        grid=(b, t // ta),
        in_specs=in_specs,
        out_specs=out_specs,
        scratch_shapes=[pltpu.VMEM((ta, ncol), F32), pltpu.VMEM((1, LANES), F32)],
        compiler_params=pltpu.CompilerParams(
            dimension_semantics=("arbitrary", "arbitrary"), vmem_limit_bytes=VMEM_LIMIT),
        name="proj",
    )(x, n1, w_pad, bf_pad, qg2, kg2, lb_logits, carry_in)


def _cache_prep_kernel(k_ref, v_ref, lf_ref, kaug_out, vaug_out, ctot_out, *, nh, dh, rb):
    t = k_ref.shape[1]

    def body(r, carry):
        rows = pl.ds(pl.multiple_of(r * rb, rb), rb)
        c = _cumsum_rows(lf_ref[0, rows, :]) + carry
        for p in range(nh // 2):
            cols = slice(LANES * p, LANES * (p + 1))
            kk = k_ref[0, rows, cols]
            vv = v_ref[0, rows, cols]
            for odd in range(2):
                hd = 2 * p + odd
                kaug_out[0, hd, rows, :] = _pack_k(kk, c, hd, dh)
                vaug_out[0, hd, rows, :] = _pack_v(vv, hd, dh)
        return c[rb - 1:rb, :]

    ctot_out[0] = lax.fori_loop(0, t // rb, body, jnp.zeros((1, LANES), F32))


def _cache_prep(ck, cv, clf_pad, *, nh, dh):
    b, t, bw = ck.shape
    rb = min(64, t)
    whole = lambda bi: (bi, 0, 0)
    whole4 = lambda bi: (bi, 0, 0, 0)
    return pl.pallas_call(
        functools.partial(_cache_prep_kernel, nh=nh, dh=dh, rb=rb),
        out_shape=(jax.ShapeDtypeStruct((b, nh, t, LANES), BF16),
                   jax.ShapeDtypeStruct((b, nh, t, LANES), BF16),
                   jax.ShapeDtypeStruct((b, 1, LANES), F32)),
        grid=(b,),
        in_specs=[pl.BlockSpec((1, t, bw), whole), pl.BlockSpec((1, t, bw), whole),
                  pl.BlockSpec((1, t, LANES), whole)],
        out_specs=(pl.BlockSpec((1, nh, t, LANES), whole4), pl.BlockSpec((1, nh, t, LANES), whole4),
                   pl.BlockSpec((1, 1, LANES), whole)),
        compiler_params=pltpu.CompilerParams(dimension_semantics=("arbitrary",)),
        name="cache_prep",
    )(ck, cv, clf_pad)


def _hgrn_kernel(q_ref, k_ref, v_ref, lf_ref, ga_ref, s0_ref, gn_ref, oa_out, s_out, st_scr,
                 *, nha, chunk):
    i = pl.program_id(1)
    th = q_ref.shape[1]

    @pl.when(i == 0)
    def _():
        for hd in range(nha):
            st_scr[hd] = s0_ref[0, hd].T

    ci = lax.broadcasted_iota(jnp.int32, (chunk, chunk), 0)
    si = lax.broadcasted_iota(jnp.int32, (chunk, chunk), 1)
    causal = ci >= si
    nt = (((1,), (1,)), ((), ()))
    tn = (((0,), (0,)), ((), ()))

    def body(cidx, _):
        rows = pl.ds(pl.multiple_of(cidx * chunk, chunk), chunk)
        for hd in range(nha):
            cols = slice(LANES * hd, LANES * (hd + 1))
            b = _cumsum_rows(lf_ref[0, rows, cols])
            b_ref = b[chunk // 2:chunk // 2 + 1, :]
            b_last = b[chunk - 1:chunk, :]
            q = q_ref[0, rows, cols].astype(F32)
            k = k_ref[0, rows, cols].astype(F32)
            v = v_ref[0, rows, cols]
            st = st_scr[hd]
            inter = lax.dot_general((q * jnp.exp(b)).astype(BF16), st.astype(BF16), nt,
                                    preferred_element_type=F32)
            qr = (q * jnp.exp(b - b_ref)).astype(BF16)
            kr = (k * jnp.exp(b_ref - b)).astype(BF16)
            a = lax.dot_general(qr, kr, nt, preferred_element_type=F32)
            a = jnp.where(causal, a, 0.0).astype(BF16)
            o = inter + jnp.dot(a, v, preferred_element_type=F32)
            k3 = (k * jnp.exp(b_last - b)).astype(BF16)
            st_scr[hd] = st * jnp.exp(b_last) + lax.dot_general(v, k3, tn, preferred_element_type=F32)
            ms = jnp.mean(o * o, axis=-1, keepdims=True)
            g = ga_ref[0, rows, cols].astype(F32)
            on = o * lax.rsqrt(ms + EPS) * gn_ref[:, cols]
            oa_out[0, rows, cols] = (on * (g * jax.nn.sigmoid(g))).astype(BF16)
        return 0

    lax.fori_loop(0, th // chunk, body, 0)

    @pl.when(i == pl.num_programs(1) - 1)
    def _():
        for hd in range(nha):
            s_out[0, hd] = st_scr[hd].T


def _hgrn(hq, hk, hv, hlf, ga, s0, gnorm, *, nha, chunk):
    b, t, aw = hq.shape
    th = min(512, t)
    dk, dv = s0.shape[2], s0.shape[3]
    tok = lambda bi, i: (bi, i, 0)
    st = lambda bi, i: (bi, 0, 0, 0)
    return pl.pallas_call(
        functools.partial(_hgrn_kernel, nha=nha, chunk=chunk),
        out_shape=(jax.ShapeDtypeStruct((b, t, aw), BF16),
                   jax.ShapeDtypeStruct((b, nha, dk, dv), F32)),
        grid=(b, t // th),
        in_specs=[pl.BlockSpec((1, th, aw), tok)] * 5
                 + [pl.BlockSpec((1, nha, dk, dv), st), _const_spec(gnorm.shape)],
        out_specs=(pl.BlockSpec((1, th, aw), tok), pl.BlockSpec((1, nha, dk, dv), st)),
        scratch_shapes=[pltpu.VMEM((nha, dv, dk), F32)],
        compiler_params=pltpu.CompilerParams(dimension_semantics=("arbitrary", "arbitrary")),
        name="hgrn",
    )(hq, hk, hv, hlf, ga, s0, gnorm)


def _flash_step(q, k, v, m_ref, acc_ref, hh, rows, mask):
    s = lax.dot_general(q, k, (((1,), (1,)), ((), ())), preferred_element_type=F32)
    if mask is not None:
        s = jnp.where(mask, s, NEG_BIG)
    m_prev = m_ref[hh, rows, :]
    m_new = jnp.maximum(m_prev, jnp.max(s, axis=-1, keepdims=True))
    alpha = jnp.exp2(m_prev - m_new)
    p = jnp.concatenate(
        [jnp.exp2(s[:, LANES * c:LANES * (c + 1)] - m_new) for c in range(s.shape[1] // LANES)], axis=1)
    acc_ref[hh, rows, :] = alpha * acc_ref[hh, rows, :] + jnp.dot(p.astype(BF16), v, preferred_element_type=F32)
    m_ref[hh, rows, :] = m_new


def _merge_pair(acc_ref, dh):
    outs = []
    for hh in range(2):
        acc = acc_ref[hh]
        outs.append(acc / jnp.broadcast_to(acc[:, dh:dh + 1], acc.shape))
    lane = lax.broadcasted_iota(jnp.int32, outs[0].shape, 1)
    return jnp.where(lane < dh, outs[0], pltpu.roll(outs[1], dh, axis=1))


def _attn_kernel(cb_ref, thr_ref, q_ref, k_ref, v_ref, o_ref, m_scr, acc_scr, *, tq, dh, nh):
    bi, p, i = pl.program_id(0), pl.program_id(1), pl.program_id(2)
    nblk = pl.num_programs(2)
    m_scr[...] = jnp.full(m_scr.shape, NEG_BIG, F32)
    acc_scr[...] = jnp.zeros(acc_scr.shape, F32)

    def first_live_block(hd):
        base = (bi * nh + hd) * nblk
        c_before = cb_ref[base + jnp.maximum(i - 1, 0)]

        def count(j, n):
            dead = (c_before - cb_ref[base + j]) * LOG2E + thr_ref[0] < EXP2_ZERO_BELOW
            return n + jnp.where(jnp.logical_and(dead, n == j), 1, 0)

        return lax.fori_loop(0, i, count, 0)

    j_lo = jnp.minimum(first_live_block(2 * p), first_live_block(2 * p + 1))

    def body(j, _):
        keys = pl.ds(pl.multiple_of(j * tq, tq), tq)
        for hh in range(2):
            _flash_step(q_ref[0, hh], k_ref[0, hh, keys, :], v_ref[0, hh, keys, :], m_scr, acc_scr, hh,
                        slice(0, tq), None)
        return 0

    lax.fori_loop(j_lo, i, body, 0)

    half = tq // 2
    keys_a = pl.ds(pl.multiple_of(i * tq, tq), half)
    keys_b = pl.ds(pl.multiple_of(i * tq + half, half), half)
    mask_a = (lax.broadcasted_iota(jnp.int32, (tq, half), 0) >= lax.broadcasted_iota(jnp.int32, (tq, half), 1))
    mask_b = (lax.broadcasted_iota(jnp.int32, (half, half), 0) >= lax.broadcasted_iota(jnp.int32, (half, half), 1))
    for hh in range(2):
        _flash_step(q_ref[0, hh], k_ref[0, hh, keys_a, :], v_ref[0, hh, keys_a, :], m_scr, acc_scr, hh,
                    slice(0, tq), mask_a)
        _flash_step(q_ref[0, hh, half:, :], k_ref[0, hh, keys_b, :], v_ref[0, hh, keys_b, :], m_scr, acc_scr, hh,
                    slice(half, tq), mask_b)
    o_ref[0] = _merge_pair(acc_scr, dh).astype(o_ref.dtype)


def _attn(cb, thr, qaug, kaug, vaug, *, dh):
    b, nh, t, _ = qaug.shape
    tq = min(ATTN_BLOCK, t)
    grid_spec = pltpu.PrefetchScalarGridSpec(
        num_scalar_prefetch=2,
        grid=(b, nh // 2, t // tq),
        in_specs=[pl.BlockSpec((1, 2, tq, LANES), lambda bi, p, i, *_: (bi, p, i, 0)),
                  pl.BlockSpec((1, 2, t, LANES), lambda bi, p, i, *_: (bi, p, 0, 0)),
                  pl.BlockSpec((1, 2, t, LANES), lambda bi, p, i, *_: (bi, p, 0, 0))],
        out_specs=pl.BlockSpec((1, tq, 2 * dh), lambda bi, p, i, *_: (bi, i, p)),
        scratch_shapes=[pltpu.VMEM((2, tq, LANES), F32), pltpu.VMEM((2, tq, LANES), F32)],
    )
    return pl.pallas_call(
        functools.partial(_attn_kernel, tq=tq, dh=dh, nh=nh),
        out_shape=jax.ShapeDtypeStruct((b, t, nh * dh), BF16),
        grid_spec=grid_spec,
        compiler_params=pltpu.CompilerParams(
            dimension_semantics=("arbitrary", "arbitrary", "arbitrary"), vmem_limit_bytes=VMEM_LIMIT),
        name="fox_attn",
    )(cb, thr, qaug, kaug, vaug)


def _sample_attn_kernel(q_ref, kc_ref, vc_ref, kn_ref, vn_ref, o_ref, acc_scr, *, nh, dh):
    s_len = q_ref.shape[2]
    ri = lax.broadcasted_iota(jnp.int32, (s_len, s_len), 0)
    ki = lax.broadcasted_iota(jnp.int32, (s_len, s_len), 1)
    nt = (((1,), (1,)), ((), ()))
    for p in range(nh // 2):
        for hh in range(2):
            hd = 2 * p + hh
            q = q_ref[0, hd]
            s1 = lax.dot_general(q, kc_ref[0, hd], nt, preferred_element_type=F32)
            s2 = lax.dot_general(q, kn_ref[0, hd], nt, preferred_element_type=F32)
            s2 = jnp.where(ri >= ki, s2, NEG_BIG)
            m = jnp.maximum(jnp.max(s1, axis=-1, keepdims=True), jnp.max(s2, axis=-1, keepdims=True))
            p1 = jnp.exp2(s1 - m).astype(BF16)
            p2 = jnp.exp2(s2 - m).astype(BF16)
            acc_scr[hh] = (jnp.dot(p1, vc_ref[0, hd], preferred_element_type=F32)
                           + jnp.dot(p2, vn_ref[0, hd], preferred_element_type=F32))
        o_ref[0, :, 2 * dh * p:2 * dh * (p + 1)] = _merge_pair(acc_scr, dh).astype(o_ref.dtype)


def _sample_attn(qaug, kaug_c, vaug_c, kaug_n, vaug_n, *, dh):
    b, nh, s_len, _ = qaug.shape
    p_len = kaug_c.shape[2]
    whole4 = lambda bi: (bi, 0, 0, 0)
    return pl.pallas_call(
        functools.partial(_sample_attn_kernel, nh=nh, dh=dh),
        out_shape=jax.ShapeDtypeStruct((b, s_len, nh * dh), BF16),
        grid=(b,),
        in_specs=[pl.BlockSpec((1, nh, s_len, LANES), whole4),
                  pl.BlockSpec((1, nh, p_len, LANES), whole4), pl.BlockSpec((1, nh, p_len, LANES), whole4),
                  pl.BlockSpec((1, nh, s_len, LANES), whole4), pl.BlockSpec((1, nh, s_len, LANES), whole4)],
        out_specs=pl.BlockSpec((1, s_len, nh * dh), lambda bi: (bi, 0, 0)),
        scratch_shapes=[pltpu.VMEM((2, s_len, LANES), F32)],
        compiler_params=pltpu.CompilerParams(dimension_semantics=("arbitrary",)),
        name="fox_sample_attn",
    )(qaug, kaug_c, vaug_c, kaug_n, vaug_n)


def _ffn_kernel(x_ref, oa_ref, ob_ref, woa_ref, wob_ref, n2_ref, wup_ref, wdn_ref, y_ref, h_scr, *, ff_chunk):
    x1 = (x_ref[...]
          + jnp.dot(oa_ref[...], woa_ref[...], preferred_element_type=F32)
          + jnp.dot(ob_ref[...], wob_ref[...], preferred_element_type=F32))
    ms = jnp.mean(x1 * x1, axis=-1, keepdims=True)
    h_scr[...] = (x1 * lax.rsqrt(ms + EPS) * n2_ref[...]).astype(BF16)
    y_ref[...] = x1
    for c in range(wup_ref.shape[1] // ff_chunk):
        cs = slice(ff_chunk * c, ff_chunk * (c + 1))
        u = jnp.maximum(jnp.dot(h_scr[...], wup_ref[:, cs], preferred_element_type=F32), 0.0)
        y_ref[...] += jnp.dot((u * u).astype(BF16), wdn_ref[cs, :], preferred_element_type=F32)


def _ffn(x2d, oa2d, ob2d, woa, wob, n2, wup, wdn):
    n, d = x2d.shape
    tc = min(512, n)
    aw, bw = oa2d.shape[1], ob2d.shape[1]
    row = lambda i: (i, 0)
    return pl.pallas_call(
        functools.partial(_ffn_kernel, ff_chunk=min(1024, wup.shape[1])),
        out_shape=jax.ShapeDtypeStruct((n, d), F32),
        grid=(n // tc,),
        in_specs=[pl.BlockSpec((tc, d), row), pl.BlockSpec((tc, aw), row), pl.BlockSpec((tc, bw), row),
                  _const_spec(woa.shape), _const_spec(wob.shape), _const_spec(n2.shape),
                  _const_spec(wup.shape), _const_spec(wdn.shape)],
        out_specs=pl.BlockSpec((tc, d), row),
        scratch_shapes=[pltpu.VMEM((tc, d), BF16)],
        compiler_params=pltpu.CompilerParams(
            dimension_semantics=("arbitrary",), vmem_limit_bytes=VMEM_LIMIT),
        name="out_ffn",
    )(x2d, oa2d, ob2d, woa, wob, n2, wup, wdn)


def kernel(x_prompt, x_sample, cache_fox_k, cache_fox_v, cache_fox_logf, state_hgrn, norm1, w_in, b_fox_f,
           q_norm_gain, k_norm_gain, hgrn_lb_logits, hgrn_out_norm, w_out, norm2, w_up, w_down):
    depth = w_in.shape[0]
    d = x_prompt.shape[-1]
    nh, dh = cache_fox_k.shape[3], cache_fox_k.shape[4]
    nha, dk, dv = state_hgrn.shape[2], state_hgrn.shape[3], state_hgrn.shape[4]
    aw, bw = nha * dk, nh * dh
    assert dk == LANES and dv == LANES and 2 * dh == LANES and nh % 2 == 0 and nh <= 8
    assert w_in.shape[2] == 4 * aw + 3 * bw + nh and aw + bw == w_out.shape[1]

    xp, xs = x_prompt, x_sample
    bp, tp, _ = xp.shape
    bs, ts, _ = xs.shape
    plen = cache_fox_k.shape[2]
    lb_logits = hgrn_lb_logits.astype(F32)

    kp, vp, lp, sp = [], [], [], []
    ksl, vsl, lsl, ssl = [], [], [], []
    for l in range(depth):
        w_pad = jnp.pad(w_in[l], ((0, 0), (0, LANES - nh))).astype(BF16)
        bf_pad = jnp.pad(b_fox_f[l].astype(F32), (0, LANES - nh)).reshape(1, LANES)
        qg2 = jnp.tile(q_norm_gain[l].astype(F32), 2).reshape(1, LANES)
        kg2 = jnp.tile(k_norm_gain[l].astype(F32), 2).reshape(1, LANES)
        n1 = norm1[l].astype(F32).reshape(1, d)
        n2 = norm2[l].astype(F32).reshape(1, d)
        gnorm = hgrn_out_norm[l].astype(F32).reshape(1, aw)
        woa = w_out[l, :aw].astype(BF16)
        wob = w_out[l, aw:].astype(BF16)
        wup = w_up[l].astype(BF16)
        wdn = w_down[l].astype(BF16)
        proj = functools.partial(_proj, layer=l, aw=aw, bw=bw, nh=nh, dh=dh)

        (k_p, v_p, lf_p, qaug, kaug, vaug, hq, hk, hv, hlf, ga, _, cends) = proj(
            xp, n1, w_pad, bf_pad, qg2, kg2, lb_logits, jnp.zeros((bp, 1, LANES), F32))
        oa, s_p = _hgrn(hq, hk, hv, hlf, ga, jnp.zeros((bp, nha, dk, dv), F32), gnorm,
                        nha=nha, chunk=min(HGRN_CHUNK, tp))
        per_blk = cends.shape[1] // (tp // min(ATTN_BLOCK, tp))
        cb = jnp.transpose(cends[:, per_blk - 1::per_blk, :nh], (0, 2, 1)).reshape(-1)
        qk_bound = dh ** 0.5 * jnp.max(jnp.abs(q_norm_gain[l])) * jnp.max(jnp.abs(k_norm_gain[l]))
        thr = (2.04 * LOG2E * qk_bound + 2.0).astype(F32).reshape(1)
        ob = _attn(cb, thr, qaug, kaug, vaug, dh=dh)
        xp = _ffn(xp.reshape(bp * tp, d), oa.reshape(bp * tp, aw), ob.reshape(bp * tp, bw),
                  woa, wob, n2, wup, wdn).reshape(bp, tp, d)
        kp.append(k_p.reshape(bp, tp, nh, dh))
        vp.append(v_p.reshape(bp, tp, nh, dh))
        lp.append(lf_p)
        sp.append(s_p.astype(x_prompt.dtype))

        clf_pad = jnp.pad(cache_fox_logf[l].astype(F32), ((0, 0), (0, 0), (0, LANES - nh)))
        kaug_c, vaug_c, c_tot = _cache_prep(
            cache_fox_k[l].astype(F32).reshape(bs, plen, bw), cache_fox_v[l].astype(F32).reshape(bs, plen, bw),
            clf_pad, nh=nh, dh=dh)
        (k_s, v_s, lf_s, qaug, kaug, vaug, hq, hk, hv, hlf, ga, _, _) = proj(
            xs, n1, w_pad, bf_pad, qg2, kg2, lb_logits, c_tot)
        oa, s_s = _hgrn(hq, hk, hv, hlf, ga, state_hgrn[l].astype(F32), gnorm, nha=nha, chunk=ts)
        ob = _sample_attn(qaug, kaug_c, vaug_c, kaug, vaug, dh=dh)
        xs = _ffn(xs.reshape(bs * ts, d), oa.reshape(bs * ts, aw), ob.reshape(bs * ts, bw),
                  woa, wob, n2, wup, wdn).reshape(bs, ts, d)
        ksl.append(k_s.reshape(bs, ts, nh, dh))
        vsl.append(v_s.reshape(bs, ts, nh, dh))
        lsl.append(lf_s)
        ssl.append(s_s.astype(state_hgrn.dtype))

    return (xp, xs, jnp.stack(kp), jnp.stack(vp), jnp.stack(lp), jnp.stack(sp),
            jnp.stack(ksl), jnp.stack(vsl), jnp.stack(lsl), jnp.stack(ssl))
```

```python
import functools

import jax
import jax.numpy as jnp
from jax import lax
from jax.experimental import pallas as pl
from jax.experimental.pallas import tpu as pltpu

F32 = jnp.float32
BF16 = jnp.bfloat16

EPS = 1e-6
HGRN_CHUNK = 64
TOKEN_TILE = 512
ROW_BLOCK = 64
ATTN_HEADS_PER_STEP = 4
LANES = 128
MXU_COLS = 256
NEG_BIG = -1e30
LOG2E = 1.4426950408889634
EXP2_ZERO_BELOW = -150.0
FIXED_OFFSET_MAX_LOGIT = 60.0
VMEM_LIMIT = 56 * 1024 * 1024

_NT = (((1,), (1,)), ((), ()))
_TN = (((0,), (0,)), ((), ()))


def _const_spec(shape):
    nd = len(shape)
    return pl.BlockSpec(shape, lambda *_: (0,) * nd, pipeline_mode=pl.Buffered(1))


def _cumsum_rows(x):
    n = x.shape[0]
    row = lax.broadcasted_iota(jnp.int32, x.shape, 0)
    s = 1
    while s < n:
        x = x + jnp.where(row >= s, pltpu.roll(x, s, axis=0), 0.0)
        s *= 2
    return x


def _pack_v(pair, odd, dh):
    lane = lax.broadcasted_iota(jnp.int32, pair.shape, 1)
    head = pltpu.roll(pair, dh, axis=1) if odd else pair
    return jnp.where(lane < dh, head, jnp.where(lane == dh, 1.0, 0.0)).astype(BF16)


def _pair_rms_norm(z, gain2, dh):
    lane = lax.broadcasted_iota(jnp.int32, z.shape, 1)
    first = lane < dh
    sq = z * z
    s0 = jnp.sum(jnp.where(first, sq, 0.0), axis=-1, keepdims=True)
    s1 = jnp.sum(jnp.where(first, 0.0, sq), axis=-1, keepdims=True)
    ms = jnp.where(first, s0, s1) * (1.0 / dh)
    return z * lax.rsqrt(ms + EPS) * gain2


def _log_sigmoid(x):
    return jnp.minimum(x, 0.0) - jnp.log1p(jnp.exp(-jnp.abs(x)))


def _c_layouts(c2, nh, hps):
    return [c2 if g == 0 else pltpu.roll(c2, LANES - hps * g, axis=1) for g in range(nh // hps)]


def _proj_kernel(x_ref, n1_ref, wf_ref, wh_ref, bf_ref, qg_ref, kg_ref, lbl_ref, cin_ref,
                 k_out, v_out, lf_out, qb_out, kb_out, vaug_out, ccol_out, crow_out,
                 hq_out, hk_out, hv_out, hlf_out, ga_out, ctot_out, cends_out,
                 h_scr, zf_scr, zh_scr, c_scr, carry_scr, *, layer, aw, bw, nh, dh, rb, hps):
    i = pl.program_id(1)

    @pl.when(i == 0)
    def _():
        carry_scr[...] = cin_ref[0]

    x = x_ref[0]
    ta = x.shape[0]
    nrb = ta // rb
    nsl = wh_ref.shape[0]
    ms = jnp.mean(x * x, axis=-1, keepdims=True)
    h_scr[...] = (x * lax.rsqrt(ms + EPS) * n1_ref[...]).astype(BF16)
    zf_scr[...] = jnp.dot(h_scr[...], wf_ref[...], preferred_element_type=F32)

    interleave = nrb == nsl
    if not interleave:
        for sl in range(nsl):
            zh_scr[sl] = jnp.dot(h_scr[...], wh_ref[sl], preferred_element_type=F32)

    scale = dh ** -0.5 * LOG2E
    rb_idx = lax.broadcasted_iota(jnp.int32, (nrb, LANES), 0)

    def fox_rows(r, state):
        carry, cends = state
        if interleave:
            zh_scr[r] = jnp.dot(h_scr[...], wh_ref[r], preferred_element_type=F32)
        rows = pl.ds(pl.multiple_of(r * rb, rb), rb)
        lf = _log_sigmoid(zf_scr[rows, 3 * bw:3 * bw + LANES] + bf_ref[...])
        lf_out[0, rows, :] = lf[:, :nh]
        c = _cumsum_rows(lf) + carry
        c2 = c * LOG2E
        c_scr[rows, :] = c2
        for g, cg in enumerate(_c_layouts(c2, nh, hps)):
            ccol_out[0, g, rows, :] = cg
        for p in range(nh // 2):
            cols = slice(LANES * p, LANES * (p + 1))
            qn = _pair_rms_norm(zf_scr[rows, LANES * p:LANES * (p + 1)], qg_ref[...], dh) * scale
            kn = _pair_rms_norm(zf_scr[rows, bw + LANES * p:bw + LANES * (p + 1)], kg_ref[...], dh)
            vv = zf_scr[rows, 2 * bw + LANES * p:2 * bw + LANES * (p + 1)]
            qb_out[0, rows, cols] = qn.astype(BF16)
            kb_out[0, rows, cols] = kn.astype(BF16)
            k_out[0, rows, cols] = kn
            v_out[0, rows, cols] = vv
            for odd in range(2):
                vaug_out[0, 2 * p + odd, rows, :] = _pack_v(vv, odd, dh)
        carry = c[rb - 1:rb, :]
        return carry, jnp.where(rb_idx == r, carry, cends)

    carry, cends = lax.fori_loop(0, nrb, fox_rows, (carry_scr[...], jnp.zeros((nrb, LANES), F32)))
    carry_scr[...] = carry
    ctot_out[0] = carry
    cends_out[0] = cends
    crow_out[0, 0] = c_scr[...].T[:nh, :]

    lg = lbl_ref[...]
    e = jnp.exp(lg - jnp.max(lg, axis=0, keepdims=True))
    lb = jnp.sum(e[:layer + 1], axis=0, keepdims=True) / jnp.sum(e, axis=0, keepdims=True)
    spw = aw // MXU_COLS

    def hgrn_rows(r, _):
        rows = pl.ds(pl.multiple_of(r * rb, rb), rb)
        for sl in range(spw):
            cols = slice(MXU_COLS * sl, MXU_COLS * (sl + 1))
            hq_out[0, rows, cols] = zh_scr[sl, rows, :].astype(BF16)
            f = lb[:, cols] + (1.0 - lb[:, cols]) * jax.nn.sigmoid(zh_scr[spw + sl, rows, :])
            hk_out[0, rows, cols] = (1.0 - f).astype(BF16)
            hlf_out[0, rows, cols] = jnp.log(f)
            hv_out[0, rows, cols] = zh_scr[2 * spw + sl, rows, :].astype(BF16)
            ga_out[0, rows, cols] = zh_scr[3 * spw + sl, rows, :].astype(BF16)
        return 0

    lax.fori_loop(0, nrb, hgrn_rows, 0)


def _proj(x, n1, w_fox, w_hgrn, bf_pad, qg2, kg2, lb_logits, carry_in, *, layer, aw, bw, nh, dh):
    b, t, d = x.shape
    ta = min(TOKEN_TILE, t)
    rb = min(ROW_BLOCK, ta)
    hps = min(ATTN_HEADS_PER_STEP, nh)
    tok = lambda bi, i: (bi, i, 0)
    tok4 = lambda bi, i: (bi, 0, i, 0)
    per_b = lambda bi, i: (bi, 0, 0)
    out_shape = (
        jax.ShapeDtypeStruct((b, t, bw), F32),
        jax.ShapeDtypeStruct((b, t, bw), F32),
        jax.ShapeDtypeStruct((b, t, nh), F32),
        jax.ShapeDtypeStruct((b, t, bw), BF16),
        jax.ShapeDtypeStruct((b, t, bw), BF16),
        jax.ShapeDtypeStruct((b, nh, t, LANES), BF16),
        jax.ShapeDtypeStruct((b, nh // hps, t, LANES), F32),
        jax.ShapeDtypeStruct((b, t // ta, nh, ta), F32),
        jax.ShapeDtypeStruct((b, t, aw), BF16),
        jax.ShapeDtypeStruct((b, t, aw), BF16),
        jax.ShapeDtypeStruct((b, t, aw), BF16),
        jax.ShapeDtypeStruct((b, t, aw), F32),
        jax.ShapeDtypeStruct((b, t, aw), BF16),
        jax.ShapeDtypeStruct((b, 1, LANES), F32),
        jax.ShapeDtypeStruct((b, t // rb, LANES), F32),
    )
    out_specs = (
        pl.BlockSpec((1, ta, bw), tok), pl.BlockSpec((1, ta, bw), tok), pl.BlockSpec((1, ta, nh), tok),
        pl.BlockSpec((1, ta, bw), tok), pl.BlockSpec((1, ta, bw), tok),
        pl.BlockSpec((1, nh, ta, LANES), tok4),
        pl.BlockSpec((1, nh // hps, ta, LANES), tok4),
        pl.BlockSpec((1, 1, nh, ta), lambda bi, i: (bi, i, 0, 0)),
        pl.BlockSpec((1, ta, aw), tok), pl.BlockSpec((1, ta, aw), tok), pl.BlockSpec((1, ta, aw), tok),
        pl.BlockSpec((1, ta, aw), tok), pl.BlockSpec((1, ta, aw), tok),
        pl.BlockSpec((1, 1, LANES), per_b),
        pl.BlockSpec((1, ta // rb, LANES), tok),
    )
    in_specs = [
        pl.BlockSpec((1, ta, d), tok),
        _const_spec(n1.shape), _const_spec(w_fox.shape), _const_spec(w_hgrn.shape), _const_spec(bf_pad.shape),
        _const_spec(qg2.shape), _const_spec(kg2.shape), _const_spec(lb_logits.shape),
        pl.BlockSpec((1, 1, LANES), per_b),
    ]
    return pl.pallas_call(
        functools.partial(_proj_kernel, layer=layer, aw=aw, bw=bw, nh=nh, dh=dh, rb=rb, hps=hps),
        out_shape=out_shape,
        grid=(b, t // ta),
        in_specs=in_specs,
        out_specs=out_specs,
        scratch_shapes=[pltpu.VMEM((ta, d), BF16), pltpu.VMEM((ta, w_fox.shape[1]), F32),
                        pltpu.VMEM((w_hgrn.shape[0], ta, MXU_COLS), F32), pltpu.VMEM((ta, LANES), F32),
                        pltpu.VMEM((1, LANES), F32)],
        compiler_params=pltpu.CompilerParams(
            dimension_semantics=("arbitrary", "arbitrary"), vmem_limit_bytes=VMEM_LIMIT),
        name="proj",
    )(x, n1, w_fox, w_hgrn, bf_pad, qg2, kg2, lb_logits, carry_in)


def _cache_prep_kernel(k_ref, v_ref, lf_ref, kb_out, vaug_out, crow_out, ctot_out, c_scr, *, nh, dh, rb):
    t = k_ref.shape[1]

    def body(r, carry):
        rows = pl.ds(pl.multiple_of(r * rb, rb), rb)
        c = _cumsum_rows(lf_ref[0, rows, :]) + carry
        c_scr[rows, :] = c * LOG2E
        kb_out[0, rows, :] = k_ref[0, rows, :].astype(BF16)
        for p in range(nh // 2):
            vv = v_ref[0, rows, LANES * p:LANES * (p + 1)]
            for odd in range(2):
                vaug_out[0, 2 * p + odd, rows, :] = _pack_v(vv, odd, dh)
        return c[rb - 1:rb, :]

    ctot_out[0] = lax.fori_loop(0, t // rb, body, jnp.zeros((1, LANES), F32))
    crow_out[0] = c_scr[...].T[:nh, :]


def _cache_prep(ck, cv, clf_pad, *, nh, dh):
    b, t, bw = ck.shape
    rb = min(ROW_BLOCK, t)
    whole = lambda bi: (bi, 0, 0)
    whole4 = lambda bi: (bi, 0, 0, 0)
    return pl.pallas_call(
        functools.partial(_cache_prep_kernel, nh=nh, dh=dh, rb=rb),
        out_shape=(jax.ShapeDtypeStruct((b, t, bw), BF16),
                   jax.ShapeDtypeStruct((b, nh, t, LANES), BF16),
                   jax.ShapeDtypeStruct((b, nh, t), F32),
                   jax.ShapeDtypeStruct((b, 1, LANES), F32)),
        grid=(b,),
        in_specs=[pl.BlockSpec((1, t, bw), whole), pl.BlockSpec((1, t, bw), whole),
                  pl.BlockSpec((1, t, LANES), whole)],
        out_specs=(pl.BlockSpec((1, t, bw), whole), pl.BlockSpec((1, nh, t, LANES), whole4),
                   pl.BlockSpec((1, nh, t), whole), pl.BlockSpec((1, 1, LANES), whole)),
        scratch_shapes=[pltpu.VMEM((t, LANES), F32)],
        compiler_params=pltpu.CompilerParams(dimension_semantics=("arbitrary",)),
        name="cache_prep",
    )(ck, cv, clf_pad)


def _hgrn_kernel(q_ref, k_ref, v_ref, lf_ref, ga_ref, s0_ref, gn_ref, oa_out, s_out, st_scr,
                 *, nha, chunk):
    i = pl.program_id(1)
    th = q_ref.shape[1]

    @pl.when(i == 0)
    def _():
        for hd in range(nha):
            st_scr[hd] = s0_ref[0, hd].T

    ci = lax.broadcasted_iota(jnp.int32, (chunk, chunk), 0)
    si = lax.broadcasted_iota(jnp.int32, (chunk, chunk), 1)
    causal = ci >= si

    def body(cidx, _):
        rows = pl.ds(pl.multiple_of(cidx * chunk, chunk), chunk)
        for hd in range(nha):
            cols = slice(LANES * hd, LANES * (hd + 1))
            b = _cumsum_rows(lf_ref[0, rows, cols])
            b_ref = b[chunk // 2:chunk // 2 + 1, :]
            b_last = b[chunk - 1:chunk, :]
            q = q_ref[0, rows, cols].astype(F32)
            k = k_ref[0, rows, cols].astype(F32)
            v = v_ref[0, rows, cols]
            st = st_scr[hd]
            inter = lax.dot_general((q * jnp.exp(b)).astype(BF16), st.astype(BF16), _NT,
                                    preferred_element_type=F32)
            qr = (q * jnp.exp(b - b_ref)).astype(BF16)
            kr = (k * jnp.exp(b_ref - b)).astype(BF16)
            a = lax.dot_general(qr, kr, _NT, preferred_element_type=F32)
            a = jnp.where(causal, a, 0.0).astype(BF16)
            o = inter + jnp.dot(a, v, preferred_element_type=F32)
            k3 = (k * jnp.exp(b_last - b)).astype(BF16)
            st_scr[hd] = st * jnp.exp(b_last) + lax.dot_general(v, k3, _TN, preferred_element_type=F32)
            ms = jnp.mean(o * o, axis=-1, keepdims=True)
            g = ga_ref[0, rows, cols].astype(F32)
            on = o * lax.rsqrt(ms + EPS) * gn_ref[:, cols]
            oa_out[0, rows, cols] = (on * (g * jax.nn.sigmoid(g))).astype(BF16)
        return 0

    nchunks = th // chunk
    lax.fori_loop(0, nchunks, body, 0, unroll=max(u for u in (4, 2, 1) if nchunks % u == 0))

    @pl.when(i == pl.num_programs(1) - 1)
    def _():
        for hd in range(nha):
            s_out[0, hd] = st_scr[hd].T


def _hgrn(hq, hk, hv, hlf, ga, s0, gnorm, *, nha, chunk):
    b, t, aw = hq.shape
    th = min(TOKEN_TILE, t)
    dk, dv = s0.shape[2], s0.shape[3]
    tok = lambda bi, i: (bi, i, 0)
    st = lambda bi, i: (bi, 0, 0, 0)
    return pl.pallas_call(
        functools.partial(_hgrn_kernel, nha=nha, chunk=chunk),
        out_shape=(jax.ShapeDtypeStruct((b, t, aw), BF16),
                   jax.ShapeDtypeStruct((b, nha, dk, dv), F32)),
        grid=(b, t // th),
        in_specs=[pl.BlockSpec((1, th, aw), tok)] * 5
                 + [pl.BlockSpec((1, nha, dk, dv), st), _const_spec(gnorm.shape)],
        out_specs=(pl.BlockSpec((1, th, aw), tok), pl.BlockSpec((1, nha, dk, dv), st)),
        scratch_shapes=[pltpu.VMEM((nha, dv, dk), F32)],
        compiler_params=pltpu.CompilerParams(dimension_semantics=("arbitrary", "arbitrary")),
        name="hgrn",
    )(hq, hk, hv, hlf, ga, s0, gnorm)


def _one_head(q_pair, odd, dh):
    lane = lax.broadcasted_iota(jnp.int32, q_pair.shape, 1)
    keep = lane >= dh if odd else lane < dh
    return jnp.where(keep, q_pair, jnp.zeros_like(q_pair))


def _flash_step(q, k, v, ck, cq, m_ref, acc_ref, hh, rows, mask):
    t = lax.dot_general(q, k, _NT, preferred_element_type=F32) - ck
    if mask is not None:
        t = jnp.where(mask, t, NEG_BIG)
    m_prev = m_ref[hh, rows, :]
    m_new = jnp.maximum(m_prev, jnp.max(t, axis=-1, keepdims=True) + cq)
    alpha = jnp.exp2(m_prev - m_new)
    off = m_new - cq
    p = jnp.concatenate(
        [jnp.exp2(t[:, LANES * c:LANES * (c + 1)] - off) for c in range(t.shape[1] // LANES)], axis=1)
    acc_ref[hh, rows, :] = alpha * acc_ref[hh, rows, :] + jnp.dot(p.astype(BF16), v, preferred_element_type=F32)
    m_ref[hh, rows, :] = m_new


def _fixed_offset_step(q, k, v, ck, cq, acc_ref, hh, rows, mask):
    s = lax.dot_general(q, k, _NT, preferred_element_type=F32)
    ps = []
    for c in range(s.shape[1] // LANES):
        cols = slice(LANES * c, LANES * (c + 1))
        t = s[:, cols] - ck[:, cols] + cq
        if mask is not None:
            t = jnp.where(mask[:, cols], t, NEG_BIG)
        ps.append(jnp.exp2(t))
    p = jnp.concatenate(ps, axis=1).astype(BF16)
    acc_ref[hh, rows, :] += jnp.dot(p, v, preferred_element_type=F32)


def _merge_pair(acc0, acc1, dh):
    o0 = acc0 / jnp.broadcast_to(acc0[:, dh:dh + 1], acc0.shape)
    o1 = acc1 / jnp.broadcast_to(acc1[:, dh:dh + 1], acc1.shape)
    lane = lax.broadcasted_iota(jnp.int32, o0.shape, 1)
    return jnp.where(lane < dh, o0, pltpu.roll(o1, dh, axis=1))


def _attn_kernel(cb_ref, thr_ref, q_ref, k_ref, v_ref, crow_ref, ccol_ref, o_ref, m_scr, acc_scr,
                 *, tq, dh, nh, hps, online):
    bi, g, i = pl.program_id(0), pl.program_id(1), pl.program_id(2)
    nblk = pl.num_programs(2)
    acc_scr[...] = jnp.zeros(acc_scr.shape, F32)
    if online:
        m_scr[...] = jnp.full(m_scr.shape, NEG_BIG, F32)

    def step(q, k, v, ck, cq, hh, rows, mask):
        if online:
            _flash_step(q, k, v, ck, cq, m_scr, acc_scr, hh, rows, mask)
        else:
            _fixed_offset_step(q, k, v, ck, cq, acc_scr, hh, rows, mask)

    def first_live_block(hd):
        base = (bi * nh + hd) * nblk
        c_before = cb_ref[base + jnp.maximum(i - 1, 0)]

        def count(j, n):
            dead = (c_before - cb_ref[base + j]) * LOG2E + thr_ref[0] < EXP2_ZERO_BELOW
            return n + jnp.where(jnp.logical_and(dead, n == j), 1, 0)

        return lax.fori_loop(0, i, count, 0)

    j_lo = first_live_block(hps * g)
    for hh in range(1, hps):
        j_lo = jnp.minimum(j_lo, first_live_block(hps * g + hh))

    def q_of(hh, rows=slice(None)):
        return _one_head(q_ref[0, rows, LANES * (hh // 2):LANES * (hh // 2 + 1)], hh % 2, dh)

    def cq_of(hh, rows=slice(None)):
        col = ccol_ref[0, 0, rows, hh:hh + 1]
        return jnp.broadcast_to(col, (col.shape[0], LANES))

    def body(j, _):
        keys = pl.ds(pl.multiple_of(j * tq, tq), tq)
        for hh in range(hps):
            kcols = slice(LANES * (hh // 2), LANES * (hh // 2 + 1))
            step(q_of(hh), k_ref[0, keys, kcols], v_ref[0, hh, keys, :], crow_ref[0, j, 0, hh:hh + 1, :],
                 cq_of(hh), hh, slice(0, tq), None)
        return 0

    lax.fori_loop(j_lo, i, body, 0)

    half = tq // 2
    keys_a = pl.ds(pl.multiple_of(i * tq, tq), half)
    keys_b = pl.ds(pl.multiple_of(i * tq + half, half), half)
    mask_a = (lax.broadcasted_iota(jnp.int32, (tq, half), 0) >= lax.broadcasted_iota(jnp.int32, (tq, half), 1))
    mask_b = (lax.broadcasted_iota(jnp.int32, (half, half), 0) >= lax.broadcasted_iota(jnp.int32, (half, half), 1))
    late = slice(half, tq)
    for hh in range(hps):
        kcols = slice(LANES * (hh // 2), LANES * (hh // 2 + 1))
        step(q_of(hh), k_ref[0, keys_a, kcols], v_ref[0, hh, keys_a, :], crow_ref[0, i, 0, hh:hh + 1, :half],
             cq_of(hh), hh, slice(0, tq), mask_a)
        step(q_of(hh, late), k_ref[0, keys_b, kcols], v_ref[0, hh, keys_b, :],
             crow_ref[0, i, 0, hh:hh + 1, half:], cq_of(hh, late), hh, late, mask_b)
    for p in range(hps // 2):
        o_ref[0, :, LANES * p:LANES * (p + 1)] = _merge_pair(
            acc_scr[2 * p], acc_scr[2 * p + 1], dh).astype(o_ref.dtype)


def _attn(cb, thr, qb, kb, vaug, crow, ccol, *, nh, dh, online):
    b, t, _ = qb.shape
    tq = min(TOKEN_TILE, t)
    hps = min(ATTN_HEADS_PER_STEP, nh)
    ng = nh // hps
    crow5 = crow.reshape(b, t // tq, ng, hps, tq)
    grid_spec = pltpu.PrefetchScalarGridSpec(
        num_scalar_prefetch=2,
        grid=(b, ng, t // tq),
        in_specs=[pl.BlockSpec((1, tq, hps * dh), lambda bi, g, i, *_: (bi, i, g)),
                  pl.BlockSpec((1, t, hps * dh), lambda bi, g, i, *_: (bi, 0, g)),
                  pl.BlockSpec((1, hps, t, LANES), lambda bi, g, i, *_: (bi, g, 0, 0)),
                  pl.BlockSpec((1, t // tq, 1, hps, tq), lambda bi, g, i, *_: (bi, 0, g, 0, 0)),
                  pl.BlockSpec((1, 1, tq, LANES), lambda bi, g, i, *_: (bi, g, i, 0))],
        out_specs=pl.BlockSpec((1, tq, hps * dh), lambda bi, g, i, *_: (bi, i, g)),
        scratch_shapes=[pltpu.VMEM((hps, tq, LANES), F32), pltpu.VMEM((hps, tq, LANES), F32)],
    )
    return pl.pallas_call(
        functools.partial(_attn_kernel, tq=tq, dh=dh, nh=nh, hps=hps, online=online),
        out_shape=jax.ShapeDtypeStruct((b, t, nh * dh), BF16),
        grid_spec=grid_spec,
        compiler_params=pltpu.CompilerParams(
            dimension_semantics=("arbitrary", "arbitrary", "arbitrary"), vmem_limit_bytes=VMEM_LIMIT),
        name="fox_attn_online" if online else "fox_attn",
    )(cb, thr, qb, kb, vaug, crow5, ccol)


def _sample_attn_kernel(q_ref, kc_ref, vc_ref, crc_ref, kn_ref, vn_ref, crn_ref, ccol_ref, o_ref, *, nh, dh, hps):
    s_len = q_ref.shape[1]
    causal = (lax.broadcasted_iota(jnp.int32, (s_len, s_len), 0)
              >= lax.broadcasted_iota(jnp.int32, (s_len, s_len), 1))
    for p in range(nh // 2):
        accs = []
        for odd in range(2):
            hd = 2 * p + odd
            cols = slice(LANES * p, LANES * (p + 1))
            q = _one_head(q_ref[0, :, cols], odd, dh)
            cq = ccol_ref[0, hd // hps, :, hd % hps:hd % hps + 1]
            t1 = lax.dot_general(q, kc_ref[0, :, cols], _NT, preferred_element_type=F32) - crc_ref[0, hd:hd + 1, :] + cq
            t2 = lax.dot_general(q, kn_ref[0, :, cols], _NT, preferred_element_type=F32) - crn_ref[0, 0, hd:hd + 1, :] + cq
            t2 = jnp.where(causal, t2, NEG_BIG)
            m = jnp.maximum(jnp.max(t1, axis=-1, keepdims=True), jnp.max(t2, axis=-1, keepdims=True))
            accs.append(jnp.dot(jnp.exp2(t1 - m).astype(BF16), vc_ref[0, hd], preferred_element_type=F32)
                        + jnp.dot(jnp.exp2(t2 - m).astype(BF16), vn_ref[0, hd], preferred_element_type=F32))
        o_ref[0, :, 2 * dh * p:2 * dh * (p + 1)] = _merge_pair(accs[0], accs[1], dh).astype(o_ref.dtype)


def _sample_attn(qb, kb_c, vaug_c, crow_c, kb_n, vaug_n, crow_n, ccol, *, nh, dh):
    b, s_len, bw = qb.shape
    p_len = kb_c.shape[1]
    hps = min(ATTN_HEADS_PER_STEP, nh)
    whole = lambda bi: (bi, 0, 0)
    whole4 = lambda bi: (bi, 0, 0, 0)
    return pl.pallas_call(
        functools.partial(_sample_attn_kernel, nh=nh, dh=dh, hps=hps),
        out_shape=jax.ShapeDtypeStruct((b, s_len, bw), BF16),
        grid=(b,),
        in_specs=[pl.BlockSpec((1, s_len, bw), whole),
                  pl.BlockSpec((1, p_len, bw), whole), pl.BlockSpec((1, nh, p_len, LANES), whole4),
                  pl.BlockSpec((1, nh, p_len), whole),
                  pl.BlockSpec((1, s_len, bw), whole), pl.BlockSpec((1, nh, s_len, LANES), whole4),
                  pl.BlockSpec((1, 1, nh, s_len), whole4),
                  pl.BlockSpec((1, nh // hps, s_len, LANES), whole4)],
        out_specs=pl.BlockSpec((1, s_len, bw), whole),
        compiler_params=pltpu.CompilerParams(dimension_semantics=("arbitrary",)),
        name="fox_sample_attn",
    )(qb, kb_c, vaug_c, crow_c, kb_n, vaug_n, crow_n, ccol)


def _ffn_kernel(x_ref, oa_ref, ob_ref, woa_ref, wob_ref, n2_ref, wup_ref, wdn_ref, y_ref, h_scr, *, ff_chunk):
    x1 = (x_ref[...]
          + jnp.dot(oa_ref[...], woa_ref[...], preferred_element_type=F32)
          + jnp.dot(ob_ref[...], wob_ref[...], preferred_element_type=F32))
    ms = jnp.mean(x1 * x1, axis=-1, keepdims=True)
    h_scr[...] = (x1 * lax.rsqrt(ms + EPS) * n2_ref[...]).astype(BF16)
    y_ref[...] = x1
    for c in range(wup_ref.shape[1] // ff_chunk):
        cs = slice(ff_chunk * c, ff_chunk * (c + 1))
        u = jnp.maximum(jnp.dot(h_scr[...], wup_ref[:, cs], preferred_element_type=F32), 0.0)
        y_ref[...] += jnp.dot((u * u).astype(BF16), wdn_ref[cs, :], preferred_element_type=F32)


def _ffn(x2d, oa2d, ob2d, woa, wob, n2, wup, wdn):
    n, d = x2d.shape
    tc = min(TOKEN_TILE, n)
    aw, bw = oa2d.shape[1], ob2d.shape[1]
    row = lambda i: (i, 0)
    return pl.pallas_call(
        functools.partial(_ffn_kernel, ff_chunk=min(1024, wup.shape[1])),
        out_shape=jax.ShapeDtypeStruct((n, d), F32),
        grid=(n // tc,),
        in_specs=[pl.BlockSpec((tc, d), row), pl.BlockSpec((tc, aw), row), pl.BlockSpec((tc, bw), row),
                  _const_spec(woa.shape), _const_spec(wob.shape), _const_spec(n2.shape),
                  _const_spec(wup.shape), _const_spec(wdn.shape)],
        out_specs=pl.BlockSpec((tc, d), row),
        scratch_shapes=[pltpu.VMEM((tc, d), BF16)],
        compiler_params=pltpu.CompilerParams(
            dimension_semantics=("arbitrary",), vmem_limit_bytes=VMEM_LIMIT),
        name="out_ffn",
    )(x2d, oa2d, ob2d, woa, wob, n2, wup, wdn)


def kernel(x_prompt, x_sample, cache_fox_k, cache_fox_v, cache_fox_logf, state_hgrn, norm1, w_in, b_fox_f,
           q_norm_gain, k_norm_gain, hgrn_lb_logits, hgrn_out_norm, w_out, norm2, w_up, w_down):
    depth = w_in.shape[0]
    d = x_prompt.shape[-1]
    nh, dh = cache_fox_k.shape[3], cache_fox_k.shape[4]
    nha, dk, dv = state_hgrn.shape[2], state_hgrn.shape[3], state_hgrn.shape[4]
    aw, bw = nha * dk, nh * dh
    assert dk == LANES and dv == LANES and 2 * dh == LANES and nh % 2 == 0 and nh <= 8
    assert w_in.shape[2] == 4 * aw + 3 * bw + nh and aw + bw == w_out.shape[1] and aw % MXU_COLS == 0

    xp, xs = x_prompt, x_sample
    bp, tp, _ = xp.shape
    bs, ts, _ = xs.shape
    plen = cache_fox_k.shape[2]
    lb_logits = hgrn_lb_logits.astype(F32)

    kp, vp, lp, sp = [], [], [], []
    ksl, vsl, lsl, ssl = [], [], [], []
    for l in range(depth):
        w_fox = jnp.pad(w_in[l, :, 4 * aw:], ((0, 0), (0, LANES - nh))).astype(BF16)
        w_hgrn = jnp.transpose(w_in[l, :, :4 * aw].reshape(d, 4 * aw // MXU_COLS, MXU_COLS), (1, 0, 2)).astype(BF16)
        bf_pad = jnp.pad(b_fox_f[l].astype(F32), (0, LANES - nh)).reshape(1, LANES)
        qg2 = jnp.tile(q_norm_gain[l].astype(F32), 2).reshape(1, LANES)
        kg2 = jnp.tile(k_norm_gain[l].astype(F32), 2).reshape(1, LANES)
        n1 = norm1[l].astype(F32).reshape(1, d)
        n2 = norm2[l].astype(F32).reshape(1, d)
        gnorm = hgrn_out_norm[l].astype(F32).reshape(1, aw)
        woa = w_out[l, :aw].astype(BF16)
        wob = w_out[l, aw:].astype(BF16)
        wup = w_up[l].astype(BF16)
        wdn = w_down[l].astype(BF16)
        proj = functools.partial(_proj, layer=l, aw=aw, bw=bw, nh=nh, dh=dh)

        (k_p, v_p, lf_p, qb, kb, vaug, ccol, crow, hq, hk, hv, hlf, ga, _, cends) = proj(
            xp, n1, w_fox, w_hgrn, bf_pad, qg2, kg2, lb_logits, jnp.zeros((bp, 1, LANES), F32))
        oa, s_p = _hgrn(hq, hk, hv, hlf, ga, jnp.zeros((bp, nha, dk, dv), F32), gnorm,
                        nha=nha, chunk=min(HGRN_CHUNK, tp))
        per_blk = cends.shape[1] // (tp // min(TOKEN_TILE, tp))
        cb = jnp.transpose(cends[:, per_blk - 1::per_blk, :nh], (0, 2, 1)).reshape(-1)
        qk_bound = dh ** 0.5 * jnp.max(jnp.abs(q_norm_gain[l])) * jnp.max(jnp.abs(k_norm_gain[l]))
        thr = (2.04 * LOG2E * qk_bound + 2.0).astype(F32).reshape(1)
        ob = lax.cond(thr[0] <= 2.0 * FIXED_OFFSET_MAX_LOGIT,
                      functools.partial(_attn, nh=nh, dh=dh, online=False),
                      functools.partial(_attn, nh=nh, dh=dh, online=True),
                      cb, thr, qb, kb, vaug, crow, ccol)
        xp = _ffn(xp.reshape(bp * tp, d), oa.reshape(bp * tp, aw), ob.reshape(bp * tp, bw),
                  woa, wob, n2, wup, wdn).reshape(bp, tp, d)
        kp.append(k_p.reshape(bp, tp, nh, dh))
        vp.append(v_p.reshape(bp, tp, nh, dh))
        lp.append(lf_p)
        sp.append(s_p.astype(x_prompt.dtype))

        clf_pad = jnp.pad(cache_fox_logf[l].astype(F32), ((0, 0), (0, 0), (0, LANES - nh)))
        kb_c, vaug_c, crow_c, c_tot = _cache_prep(
            cache_fox_k[l].astype(F32).reshape(bs, plen, bw), cache_fox_v[l].astype(F32).reshape(bs, plen, bw),
            clf_pad, nh=nh, dh=dh)
        (k_s, v_s, lf_s, qb, kb, vaug, ccol, crow, hq, hk, hv, hlf, ga, _, _) = proj(
            xs, n1, w_fox, w_hgrn, bf_pad, qg2, kg2, lb_logits, c_tot)
        oa, s_s = _hgrn(hq, hk, hv, hlf, ga, state_hgrn[l].astype(F32), gnorm, nha=nha, chunk=ts)
        ob = _sample_attn(qb, kb_c, vaug_c, crow_c, kb, vaug, crow, ccol, nh=nh, dh=dh)
        xs = _ffn(xs.reshape(bs * ts, d), oa.reshape(bs * ts, aw), ob.reshape(bs * ts, bw),
                  woa, wob, n2, wup, wdn).reshape(bs, ts, d)
        ksl.append(k_s.reshape(bs, ts, nh, dh))
        vsl.append(v_s.reshape(bs, ts, nh, dh))
        lsl.append(lf_s)
        ssl.append(s_s.astype(state_hgrn.dtype))

    return (xp, xs, jnp.stack(kp), jnp.stack(vp), jnp.stack(lp), jnp.stack(sp),
            jnp.stack(ksl), jnp.stack(vsl), jnp.stack(lsl), jnp.stack(ssl))
```

```python
import functools

import jax
import jax.numpy as jnp
from jax import lax
from jax.experimental import pallas as pl
from jax.experimental.pallas import tpu as pltpu

F32 = jnp.float32
BF16 = jnp.bfloat16

EPS = 1e-6
HGRN_CHUNK = 64
TOKEN_TILE = 512
ATTN_HEADS_PER_STEP = 4
LANES = 128
MXU_COLS = 256
NEG_BIG = -1e30
LOG2E = 1.4426950408889634
EXP2_ZERO_BELOW = -150.0
FIXED_OFFSET_MAX_LOGIT = 60.0
VMEM_LIMIT = 56 * 1024 * 1024

_NT = (((1,), (1,)), ((), ()))
_TN = (((0,), (0,)), ((), ()))


def _const_spec(shape):
    nd = len(shape)
    return pl.BlockSpec(shape, lambda *_: (0,) * nd, pipeline_mode=pl.Buffered(1))


def _cumsum_rows(x):
    n = x.shape[0]
    row = lax.broadcasted_iota(jnp.int32, x.shape, 0)
    s = 1
    while s < n:
        x = x + jnp.where(row >= s, pltpu.roll(x, s, axis=0), 0.0)
        s *= 2
    return x


def _pair_rms_norm(z, gain2, dh):
    lane = lax.broadcasted_iota(jnp.int32, z.shape, 1)
    first = lane < dh
    sq = z * z
    s0 = jnp.sum(jnp.where(first, sq, 0.0), axis=-1, keepdims=True)
    s1 = jnp.sum(jnp.where(first, 0.0, sq), axis=-1, keepdims=True)
    ms = jnp.where(first, s0, s1) * (1.0 / dh)
    return z * lax.rsqrt(ms + EPS) * gain2


def _log_sigmoid(x):
    return jnp.minimum(x, 0.0) - jnp.log1p(jnp.exp(-jnp.abs(x)))


def _hgrn_chunk(q, fa, ia, ga, lb, gn, st_ref, hd, causal):
    chunk = q.shape[0]
    f = lb + (1.0 - lb) * jax.nn.sigmoid(fa)
    k = 1.0 - f
    b = _cumsum_rows(jnp.log(f))
    b_ref = b[chunk // 2:chunk // 2 + 1, :]
    b_last = b[chunk - 1:chunk, :]
    v = ia.astype(BF16)
    st = st_ref[hd]
    inter = lax.dot_general((q * jnp.exp(b)).astype(BF16), st.astype(BF16), _NT, preferred_element_type=F32)
    qr = (q * jnp.exp(b - b_ref)).astype(BF16)
    kr = (k * jnp.exp(b_ref - b)).astype(BF16)
    a = lax.dot_general(qr, kr, _NT, preferred_element_type=F32)
    a = jnp.where(causal, a, 0.0).astype(BF16)
    o = inter + jnp.dot(a, v, preferred_element_type=F32)
    k3 = (k * jnp.exp(b_last - b)).astype(BF16)
    st_ref[hd] = st * jnp.exp(b_last) + lax.dot_general(v, k3, _TN, preferred_element_type=F32)
    ms = jnp.mean(o * o, axis=-1, keepdims=True)
    return (o * lax.rsqrt(ms + EPS) * gn * (ga * jax.nn.sigmoid(ga))).astype(BF16)


def _mixer_in_kernel(x_ref, n1_ref, wf_ref, wh_ref, bf_ref, qg_ref, kg_ref, lbl_ref, gn_ref, cin_ref, s0_ref,
                     k_out, v_out, lf_out, qb_out, kb_out, vb_out, ccol_out, crow_out, ctot_out, cends_out,
                     oa_out, s_out,
                     h_scr, zf_scr, zh_even, zh_odd, c_scr, carry_scr, st_scr,
                     *, layer, aw, bw, nh, dh, nha, rb, tiles_per_seq, n_tiles):
    t = pl.program_id(0)
    ip = jnp.minimum(t, n_tiles - 1) % tiles_per_seq
    ih = jnp.maximum(t - 1, 0) % tiles_per_seq

    @pl.when(t == 0)
    def _():
        zh_odd[...] = jnp.zeros(zh_odd.shape, F32)
        st_scr[...] = jnp.zeros(st_scr.shape, F32)

    @pl.when(jnp.logical_and(ip == 0, t < n_tiles))
    def _():
        carry_scr[...] = cin_ref[0]

    @pl.when(jnp.logical_and(ih == 0, t >= 1))
    def _():
        for hd in range(nha):
            st_scr[hd] = s0_ref[0, hd].T

    x = x_ref[0]
    ta = x.shape[0]
    nrb = ta // rb
    nsl = wh_ref.shape[0]
    ms = jnp.mean(x * x, axis=-1, keepdims=True)
    h_scr[...] = (x * lax.rsqrt(ms + EPS) * n1_ref[...]).astype(BF16)
    zf_scr[...] = jnp.dot(h_scr[...], wf_ref[...], preferred_element_type=F32)

    interleave = nrb == nsl

    lg = lbl_ref[...]
    e = jnp.exp(lg - jnp.max(lg, axis=0, keepdims=True))
    lb = jnp.sum(e[:layer + 1], axis=0, keepdims=True) / jnp.sum(e, axis=0, keepdims=True)

    scale = dh ** -0.5 * LOG2E
    rb_idx = lax.broadcasted_iota(jnp.int32, (nrb, LANES), 0)
    causal = (lax.broadcasted_iota(jnp.int32, (rb, rb), 0) >= lax.broadcasted_iota(jnp.int32, (rb, rb), 1))
    spw = aw // MXU_COLS
    hps = MXU_COLS // LANES

    def trip(zh_new, zh_old, r, state):
        carry, cends = state
        rows = pl.ds(pl.multiple_of(r * rb, rb), rb)
        if interleave:
            zh_new[r] = jnp.dot(h_scr[...], wh_ref[r], preferred_element_type=F32)

        lf = _log_sigmoid(zf_scr[rows, 3 * bw:3 * bw + LANES] + bf_ref[...])
        lf_out[0, rows, :] = lf[:, :nh]
        c = _cumsum_rows(lf) + carry
        c2 = c * LOG2E
        c_scr[rows, :] = c2
        ccol_out[0, rows, :] = c2
        for p in range(nh // 2):
            cols = slice(LANES * p, LANES * (p + 1))
            qn = _pair_rms_norm(zf_scr[rows, LANES * p:LANES * (p + 1)], qg_ref[...], dh) * scale
            kn = _pair_rms_norm(zf_scr[rows, bw + LANES * p:bw + LANES * (p + 1)], kg_ref[...], dh)
            vv = zf_scr[rows, 2 * bw + LANES * p:2 * bw + LANES * (p + 1)]
            qb_out[0, rows, cols] = qn.astype(BF16)
            kb_out[0, rows, cols] = kn.astype(BF16)
            vb_out[0, rows, cols] = vv.astype(BF16)
            k_out[0, rows, cols] = kn
            v_out[0, rows, cols] = vv

        for hd in range(nha):
            sl, ln = hd // hps, slice(LANES * (hd % hps), LANES * (hd % hps + 1))
            cols = slice(LANES * hd, LANES * (hd + 1))
            oa_out[0, rows, cols] = _hgrn_chunk(
                zh_old[sl, rows, ln], zh_old[spw + sl, rows, ln],
                zh_old[2 * spw + sl, rows, ln], zh_old[3 * spw + sl, rows, ln],
                lb[:, cols], gn_ref[:, cols], st_scr, hd, causal)

        carry = c[rb - 1:rb, :]
        return carry, jnp.where(rb_idx == r, carry, cends)

    def run(zh_new, zh_old):
        if not interleave:
            for sl in range(nsl):
                zh_new[sl] = jnp.dot(h_scr[...], wh_ref[sl], preferred_element_type=F32)
        carry, cends = lax.fori_loop(0, nrb, functools.partial(trip, zh_new, zh_old),
                                     (carry_scr[...], jnp.zeros((nrb, LANES), F32)),
                                     unroll=max(u for u in (4, 2, 1) if nrb % u == 0))
        ctot_out[0] = carry
        cends_out[0] = cends

        @pl.when(t < n_tiles - 1)
        def _():
            carry_scr[...] = carry

    pl.when(t % 2 == 0)(lambda: run(zh_even, zh_odd))
    pl.when(t % 2 == 1)(lambda: run(zh_odd, zh_even))
    crow_out[0, 0] = c_scr[...].T[:nh, :]

    @pl.when(ih == tiles_per_seq - 1)
    def _():
        for hd in range(nha):
            s_out[0, hd] = st_scr[hd].T


def _mixer_in(x, n1, w_fox, w_hgrn, bf_pad, qg2, kg2, lb_logits, gnorm, carry_in, s0, *, layer, nh, dh, chunk):
    b, t, d = x.shape
    nha, dk, dv = s0.shape[1], s0.shape[2], s0.shape[3]
    aw, bw = nha * dk, nh * dh
    ta = min(TOKEN_TILE, t)
    rb = chunk
    tps = t // ta
    n_tiles = b * tps
    assert ta % rb == 0 and t % ta == 0

    def proj_side(*tail):
        return lambda s: (jnp.minimum(s, n_tiles - 1) // tps, jnp.minimum(s, n_tiles - 1) % tps) + tail

    def rec_side(*tail):
        return lambda s: (jnp.maximum(s - 1, 0) // tps, jnp.maximum(s - 1, 0) % tps) + tail

    proj_seq = lambda s: (jnp.minimum(s, n_tiles - 1) // tps, 0, 0)
    rec_seq = lambda s: (jnp.maximum(s - 1, 0) // tps, 0, 0, 0)
    tok = proj_side(0)
    out_shape = (
        jax.ShapeDtypeStruct((b, t, bw), F32),
        jax.ShapeDtypeStruct((b, t, bw), F32),
        jax.ShapeDtypeStruct((b, t, nh), F32),
        jax.ShapeDtypeStruct((b, t, bw), BF16),
        jax.ShapeDtypeStruct((b, t, bw), BF16),
        jax.ShapeDtypeStruct((b, t, bw), BF16),
        jax.ShapeDtypeStruct((b, t, LANES), F32),
        jax.ShapeDtypeStruct((b, tps, nh, ta), F32),
        jax.ShapeDtypeStruct((b, 1, LANES), F32),
        jax.ShapeDtypeStruct((b, t // rb, LANES), F32),
        jax.ShapeDtypeStruct((b, t, aw), BF16),
        jax.ShapeDtypeStruct((b, nha, dk, dv), F32),
    )
    out_specs = (
        pl.BlockSpec((1, ta, bw), tok), pl.BlockSpec((1, ta, bw), tok), pl.BlockSpec((1, ta, nh), tok),
        pl.BlockSpec((1, ta, bw), tok), pl.BlockSpec((1, ta, bw), tok), pl.BlockSpec((1, ta, bw), tok),
        pl.BlockSpec((1, ta, LANES), tok),
        pl.BlockSpec((1, 1, nh, ta), proj_side(0, 0)),
        pl.BlockSpec((1, 1, LANES), proj_seq),
        pl.BlockSpec((1, ta // rb, LANES), tok),
        pl.BlockSpec((1, ta, aw), rec_side(0)),
        pl.BlockSpec((1, nha, dk, dv), rec_seq),
    )
    in_specs = [
        pl.BlockSpec((1, ta, d), tok),
        _const_spec(n1.shape), _const_spec(w_fox.shape), _const_spec(w_hgrn.shape), _const_spec(bf_pad.shape),
        _const_spec(qg2.shape), _const_spec(kg2.shape), _const_spec(lb_logits.shape), _const_spec(gnorm.shape),
        pl.BlockSpec((1, 1, LANES), proj_seq),
        pl.BlockSpec((1, nha, dk, dv), rec_seq),
    ]
    nsl = w_hgrn.shape[0]
    return pl.pallas_call(
        functools.partial(_mixer_in_kernel, layer=layer, aw=aw, bw=bw, nh=nh, dh=dh, nha=nha, rb=rb,
                          tiles_per_seq=tps, n_tiles=n_tiles),
        out_shape=out_shape,
        grid=(n_tiles + 1,),
        in_specs=in_specs,
        out_specs=out_specs,
        scratch_shapes=[pltpu.VMEM((ta, d), BF16), pltpu.VMEM((ta, w_fox.shape[1]), F32),
                        pltpu.VMEM((nsl, ta, MXU_COLS), F32), pltpu.VMEM((nsl, ta, MXU_COLS), F32),
                        pltpu.VMEM((ta, LANES), F32),
                        pltpu.VMEM((1, LANES), F32), pltpu.VMEM((nha, dv, dk), F32)],
        compiler_params=pltpu.CompilerParams(dimension_semantics=("arbitrary",), vmem_limit_bytes=VMEM_LIMIT),
        name="mixer_in",
    )(x, n1, w_fox, w_hgrn, bf_pad, qg2, kg2, lb_logits, gnorm, carry_in, s0)


def _cache_prep_kernel(k_ref, v_ref, lf_ref, kb_out, vb_out, crow_out, ctot_out, c_scr, *, nh, rb):
    t = k_ref.shape[1]

    def body(r, carry):
        rows = pl.ds(pl.multiple_of(r * rb, rb), rb)
        c = _cumsum_rows(lf_ref[0, rows, :]) + carry
        c_scr[rows, :] = c * LOG2E
        kb_out[0, rows, :] = k_ref[0, rows, :].astype(BF16)
        vb_out[0, rows, :] = v_ref[0, rows, :].astype(BF16)
        return c[rb - 1:rb, :]

    ctot_out[0] = lax.fori_loop(0, t // rb, body, jnp.zeros((1, LANES), F32))
    crow_out[0] = c_scr[...].T[:nh, :]


def _cache_prep(ck, cv, clf_pad, *, nh):
    b, t, bw = ck.shape
    rb = min(HGRN_CHUNK, t)
    whole = lambda bi: (bi, 0, 0)
    return pl.pallas_call(
        functools.partial(_cache_prep_kernel, nh=nh, rb=rb),
        out_shape=(jax.ShapeDtypeStruct((b, t, bw), BF16),
                   jax.ShapeDtypeStruct((b, t, bw), BF16),
                   jax.ShapeDtypeStruct((b, nh, t), F32),
                   jax.ShapeDtypeStruct((b, 1, LANES), F32)),
        grid=(b,),
        in_specs=[pl.BlockSpec((1, t, bw), whole), pl.BlockSpec((1, t, bw), whole),
                  pl.BlockSpec((1, t, LANES), whole)],
        out_specs=(pl.BlockSpec((1, t, bw), whole), pl.BlockSpec((1, t, bw), whole),
                   pl.BlockSpec((1, nh, t), whole), pl.BlockSpec((1, 1, LANES), whole)),
        scratch_shapes=[pltpu.VMEM((t, LANES), F32)],
        compiler_params=pltpu.CompilerParams(dimension_semantics=("arbitrary",)),
        name="cache_prep",
    )(ck, cv, clf_pad)


def _one_head(q_pair, odd, dh):
    lane = lax.broadcasted_iota(jnp.int32, q_pair.shape, 1)
    keep = lane >= dh if odd else lane < dh
    return jnp.where(keep, q_pair, jnp.zeros_like(q_pair))


def _head_column(c_block, lane_idx):
    lane = lax.broadcasted_iota(jnp.int32, c_block.shape, 1)
    col = jnp.sum(jnp.where(lane == lane_idx, c_block, 0.0), axis=-1, keepdims=True)
    return jnp.broadcast_to(col, c_block.shape)


def _flash_step(q, k, v, ck, cq, m_ref, l_ref, acc_ref, hh, rows, mask):
    t = lax.dot_general(q, k, _NT, preferred_element_type=F32) - ck
    if mask is not None:
        t = jnp.where(mask, t, NEG_BIG)
    m_prev = m_ref[hh, rows, :]
    m_new = jnp.maximum(m_prev, jnp.max(t, axis=-1, keepdims=True) + cq)
    alpha = jnp.exp2(m_prev - m_new)
    off = m_new - cq
    ps = [jnp.exp2(t[:, LANES * c:LANES * (c + 1)] - off) for c in range(t.shape[1] // LANES)]
    l_ref[hh, rows, :] = alpha * l_ref[hh, rows, :] + sum(ps[1:], ps[0])
    p = jnp.concatenate(ps, axis=1).astype(BF16)
    acc_ref[hh, rows, :] = alpha * acc_ref[hh, rows, :] + jnp.dot(p, v, preferred_element_type=F32)
    m_ref[hh, rows, :] = m_new


def _fixed_offset_step(q, k, v, ck, cq, l_ref, acc_ref, hh, rows, mask):
    s = lax.dot_general(q, k, _NT, preferred_element_type=F32)
    ps = []
    for c in range(s.shape[1] // LANES):
        cols = slice(LANES * c, LANES * (c + 1))
        t = s[:, cols] - ck[:, cols] + cq
        if mask is not None:
            t = jnp.where(mask[:, cols], t, NEG_BIG)
        ps.append(jnp.exp2(t))
    l_ref[hh, rows, :] += sum(ps[1:], ps[0])
    p = jnp.concatenate(ps, axis=1).astype(BF16)
    acc_ref[hh, rows, :] += jnp.dot(p, v, preferred_element_type=F32)


def _merge_pair(acc0, l0, acc1, l1, dh):
    o0 = acc0 / jnp.sum(l0, axis=-1, keepdims=True)
    o1 = acc1 / jnp.sum(l1, axis=-1, keepdims=True)
    lane = lax.broadcasted_iota(jnp.int32, o0.shape, 1)
    return jnp.where(lane < dh, o0, o1)


def _attn_kernel(cb_ref, thr_ref, q_ref, k_ref, v_ref, crow_ref, ccol_ref, o_ref, m_scr, l_scr, acc_scr,
                 *, tq, dh, nh, hps, online):
    bi, g, i = pl.program_id(0), pl.program_id(1), pl.program_id(2)
    nblk = pl.num_programs(2)
    acc_scr[...] = jnp.zeros(acc_scr.shape, F32)
    l_scr[...] = jnp.zeros(l_scr.shape, F32)
    if online:
        m_scr[...] = jnp.full(m_scr.shape, NEG_BIG, F32)

    def step(q, k, v, ck, cq, hh, rows, mask):
        if online:
            _flash_step(q, k, v, ck, cq, m_scr, l_scr, acc_scr, hh, rows, mask)
        else:
            _fixed_offset_step(q, k, v, ck, cq, l_scr, acc_scr, hh, rows, mask)

    def first_live_block(hd):
        base = (bi * nh + hd) * nblk
        c_before = cb_ref[base + jnp.maximum(i - 1, 0)]

        def count(j, n):
            dead = (c_before - cb_ref[base + j]) * LOG2E + thr_ref[0] < EXP2_ZERO_BELOW
            return n + jnp.where(jnp.logical_and(dead, n == j), 1, 0)

        return lax.fori_loop(0, i, count, 0)

    j_lo = first_live_block(hps * g)
    for hh in range(1, hps):
        j_lo = jnp.minimum(j_lo, first_live_block(hps * g + hh))

    def q_of(hh, rows=slice(None)):
        return _one_head(q_ref[0, rows, LANES * (hh // 2):LANES * (hh // 2 + 1)], hh % 2, dh)

    def cq_of(hh, rows=slice(None)):
        return _head_column(ccol_ref[0, rows, :], hps * g + hh)

    def body(j, _):
        keys = pl.ds(pl.multiple_of(j * tq, tq), tq)
        for hh in range(hps):
            kcols = slice(LANES * (hh // 2), LANES * (hh // 2 + 1))
            step(q_of(hh), k_ref[0, keys, kcols], v_ref[0, keys, kcols], crow_ref[0, j, 0, hh:hh + 1, :],
                 cq_of(hh), hh, slice(0, tq), None)
        return 0

    lax.fori_loop(j_lo, i, body, 0)

    half = tq // 2
    keys_a = pl.ds(pl.multiple_of(i * tq, tq), half)
    keys_b = pl.ds(pl.multiple_of(i * tq + half, half), half)
    mask_a = (lax.broadcasted_iota(jnp.int32, (tq, half), 0) >= lax.broadcasted_iota(jnp.int32, (tq, half), 1))
    mask_b = (lax.broadcasted_iota(jnp.int32, (half, half), 0) >= lax.broadcasted_iota(jnp.int32, (half, half), 1))
    late = slice(half, tq)
    for hh in range(hps):
        kcols = slice(LANES * (hh // 2), LANES * (hh // 2 + 1))
        step(q_of(hh), k_ref[0, keys_a, kcols], v_ref[0, keys_a, kcols], crow_ref[0, i, 0, hh:hh + 1, :half],
             cq_of(hh), hh, slice(0, tq), mask_a)
        step(q_of(hh, late), k_ref[0, keys_b, kcols], v_ref[0, keys_b, kcols],
             crow_ref[0, i, 0, hh:hh + 1, half:], cq_of(hh, late), hh, late, mask_b)
    for p in range(hps // 2):
        o_ref[0, :, LANES * p:LANES * (p + 1)] = _merge_pair(
            acc_scr[2 * p], l_scr[2 * p], acc_scr[2 * p + 1], l_scr[2 * p + 1], dh).astype(o_ref.dtype)


def _attn(cb, thr, qb, kb, vb, crow, ccol, *, nh, dh, online):
    b, t, _ = qb.shape
    tq = min(TOKEN_TILE, t)
    hps = min(ATTN_HEADS_PER_STEP, nh)
    ng = nh // hps
    crow5 = crow.reshape(b, t // tq, ng, hps, tq)
    grid_spec = pltpu.PrefetchScalarGridSpec(
        num_scalar_prefetch=2,
        grid=(b, ng, t // tq),
        in_specs=[pl.BlockSpec((1, tq, hps * dh), lambda bi, g, i, *_: (bi, i, g)),
                  pl.BlockSpec((1, t, hps * dh), lambda bi, g, i, *_: (bi, 0, g)),
                  pl.BlockSpec((1, t, hps * dh), lambda bi, g, i, *_: (bi, 0, g)),
                  pl.BlockSpec((1, t // tq, 1, hps, tq), lambda bi, g, i, *_: (bi, 0, g, 0, 0)),
                  pl.BlockSpec((1, tq, LANES), lambda bi, g, i, *_: (bi, i, 0))],
        out_specs=pl.BlockSpec((1, tq, hps * dh), lambda bi, g, i, *_: (bi, i, g)),
        scratch_shapes=[pltpu.VMEM((hps, tq, LANES), F32), pltpu.VMEM((hps, tq, LANES), F32),
                        pltpu.VMEM((hps, tq, LANES), F32)],
    )
    return pl.pallas_call(
        functools.partial(_attn_kernel, tq=tq, dh=dh, nh=nh, hps=hps, online=online),
        out_shape=jax.ShapeDtypeStruct((b, t, nh * dh), BF16),
        grid_spec=grid_spec,
        compiler_params=pltpu.CompilerParams(
            dimension_semantics=("arbitrary", "arbitrary", "arbitrary"), vmem_limit_bytes=VMEM_LIMIT),
        name="fox_attn_online" if online else "fox_attn",
    )(cb, thr, qb, kb, vb, crow5, ccol)


def _sample_attn_kernel(q_ref, kc_ref, vc_ref, crc_ref, kn_ref, vn_ref, crn_ref, ccol_ref, o_ref, *, nh, dh):
    s_len = q_ref.shape[1]
    causal = (lax.broadcasted_iota(jnp.int32, (s_len, s_len), 0)
              >= lax.broadcasted_iota(jnp.int32, (s_len, s_len), 1))
    lane = lax.broadcasted_iota(jnp.int32, (s_len, LANES), 1)
    for p in range(nh // 2):
        outs = []
        for odd in range(2):
            hd = 2 * p + odd
            cols = slice(LANES * p, LANES * (p + 1))
            q = _one_head(q_ref[0, :, cols], odd, dh)
            cq = ccol_ref[0, :, hd:hd + 1]
            t1 = lax.dot_general(q, kc_ref[0, :, cols], _NT, preferred_element_type=F32) - crc_ref[0, hd:hd + 1, :] + cq
            t2 = lax.dot_general(q, kn_ref[0, :, cols], _NT, preferred_element_type=F32) - crn_ref[0, 0, hd:hd + 1, :] + cq
            t2 = jnp.where(causal, t2, NEG_BIG)
            m = jnp.maximum(jnp.max(t1, axis=-1, keepdims=True), jnp.max(t2, axis=-1, keepdims=True))
            p1 = jnp.exp2(t1 - m)
            p2 = jnp.exp2(t2 - m)
            den = jnp.sum(p1, axis=-1, keepdims=True) + jnp.sum(p2, axis=-1, keepdims=True)
            acc = (jnp.dot(p1.astype(BF16), vc_ref[0, :, cols], preferred_element_type=F32)
                   + jnp.dot(p2.astype(BF16), vn_ref[0, :, cols], preferred_element_type=F32))
            outs.append(acc / den)
        o_ref[0, :, 2 * dh * p:2 * dh * (p + 1)] = jnp.where(lane < dh, outs[0], outs[1]).astype(o_ref.dtype)


def _sample_attn(qb, kb_c, vb_c, crow_c, kb_n, vb_n, crow_n, ccol, *, nh, dh):
    b, s_len, bw = qb.shape
    p_len = kb_c.shape[1]
    whole = lambda bi: (bi, 0, 0)
    whole4 = lambda bi: (bi, 0, 0, 0)
    return pl.pallas_call(
        functools.partial(_sample_attn_kernel, nh=nh, dh=dh),
        out_shape=jax.ShapeDtypeStruct((b, s_len, bw), BF16),
        grid=(b,),
        in_specs=[pl.BlockSpec((1, s_len, bw), whole),
                  pl.BlockSpec((1, p_len, bw), whole), pl.BlockSpec((1, p_len, bw), whole),
                  pl.BlockSpec((1, nh, p_len), whole),
                  pl.BlockSpec((1, s_len, bw), whole), pl.BlockSpec((1, s_len, bw), whole),
                  pl.BlockSpec((1, 1, nh, s_len), whole4),
                  pl.BlockSpec((1, s_len, LANES), whole)],
        out_specs=pl.BlockSpec((1, s_len, bw), whole),
        compiler_params=pltpu.CompilerParams(dimension_semantics=("arbitrary",)),
        name="fox_sample_attn",
    )(qb, kb_c, vb_c, crow_c, kb_n, vb_n, crow_n, ccol)


def _ffn_kernel(x_ref, oa_ref, ob_ref, woa_ref, wob_ref, n2_ref, wup_ref, wdn_ref, y_ref, h_scr, *, ff_chunk):
    x1 = (x_ref[...]
          + jnp.dot(oa_ref[...], woa_ref[...], preferred_element_type=F32)
          + jnp.dot(ob_ref[...], wob_ref[...], preferred_element_type=F32))
    ms = jnp.mean(x1 * x1, axis=-1, keepdims=True)
    h_scr[...] = (x1 * lax.rsqrt(ms + EPS) * n2_ref[...]).astype(BF16)
    y_ref[...] = x1
    for c in range(wup_ref.shape[1] // ff_chunk):
        cs = slice(ff_chunk * c, ff_chunk * (c + 1))
        u = jnp.maximum(jnp.dot(h_scr[...], wup_ref[:, cs], preferred_element_type=F32), 0.0)
        y_ref[...] += jnp.dot((u * u).astype(BF16), wdn_ref[cs, :], preferred_element_type=F32)


def _ffn(x2d, oa2d, ob2d, woa, wob, n2, wup, wdn):
    n, d = x2d.shape
    tc = min(TOKEN_TILE, n)
    aw, bw = oa2d.shape[1], ob2d.shape[1]
    row = lambda i: (i, 0)
    return pl.pallas_call(
        functools.partial(_ffn_kernel, ff_chunk=min(1024, wup.shape[1])),
        out_shape=jax.ShapeDtypeStruct((n, d), F32),
        grid=(n // tc,),
        in_specs=[pl.BlockSpec((tc, d), row), pl.BlockSpec((tc, aw), row), pl.BlockSpec((tc, bw), row),
                  _const_spec(woa.shape), _const_spec(wob.shape), _const_spec(n2.shape),
                  _const_spec(wup.shape), _const_spec(wdn.shape)],
        out_specs=pl.BlockSpec((tc, d), row),
        scratch_shapes=[pltpu.VMEM((tc, d), BF16)],
        compiler_params=pltpu.CompilerParams(
            dimension_semantics=("arbitrary",), vmem_limit_bytes=VMEM_LIMIT),
        name="out_ffn",
    )(x2d, oa2d, ob2d, woa, wob, n2, wup, wdn)


def kernel(x_prompt, x_sample, cache_fox_k, cache_fox_v, cache_fox_logf, state_hgrn, norm1, w_in, b_fox_f,
           q_norm_gain, k_norm_gain, hgrn_lb_logits, hgrn_out_norm, w_out, norm2, w_up, w_down):
    depth = w_in.shape[0]
    d = x_prompt.shape[-1]
    nh, dh = cache_fox_k.shape[3], cache_fox_k.shape[4]
    nha, dk, dv = state_hgrn.shape[2], state_hgrn.shape[3], state_hgrn.shape[4]
    aw, bw = nha * dk, nh * dh
    assert dk == LANES and dv == LANES and 2 * dh == LANES and nh % 2 == 0 and nh <= 8
    assert w_in.shape[2] == 4 * aw + 3 * bw + nh and aw + bw == w_out.shape[1] and aw % MXU_COLS == 0

    xp, xs = x_prompt, x_sample
    bp, tp, _ = xp.shape
    bs, ts, _ = xs.shape
    plen = cache_fox_k.shape[2]
    lb_logits = hgrn_lb_logits.astype(F32)

    kp, vp, lp, sp = [], [], [], []
    ksl, vsl, lsl, ssl = [], [], [], []
    for l in range(depth):
        w_fox = jnp.pad(w_in[l, :, 4 * aw:], ((0, 0), (0, LANES - nh))).astype(BF16)
        w_hgrn = jnp.transpose(w_in[l, :, :4 * aw].reshape(d, 4 * aw // MXU_COLS, MXU_COLS), (1, 0, 2)).astype(BF16)
        bf_pad = jnp.pad(b_fox_f[l].astype(F32), (0, LANES - nh)).reshape(1, LANES)
        qg2 = jnp.tile(q_norm_gain[l].astype(F32), 2).reshape(1, LANES)
        kg2 = jnp.tile(k_norm_gain[l].astype(F32), 2).reshape(1, LANES)
        n1 = norm1[l].astype(F32).reshape(1, d)
        n2 = norm2[l].astype(F32).reshape(1, d)
        gnorm = hgrn_out_norm[l].astype(F32).reshape(1, aw)
        woa = w_out[l, :aw].astype(BF16)
        wob = w_out[l, aw:].astype(BF16)
        wup = w_up[l].astype(BF16)
        wdn = w_down[l].astype(BF16)
        mixer_in = functools.partial(_mixer_in, layer=l, nh=nh, dh=dh)

        (k_p, v_p, lf_p, qb, kb, vb, ccol, crow, _, cends, oa, s_p) = mixer_in(
            xp, n1, w_fox, w_hgrn, bf_pad, qg2, kg2, lb_logits, gnorm, jnp.zeros((bp, 1, LANES), F32),
            jnp.zeros((bp, nha, dk, dv), F32), chunk=min(HGRN_CHUNK, tp))
        per_blk = cends.shape[1] // (tp // min(TOKEN_TILE, tp))
        cb = jnp.transpose(cends[:, per_blk - 1::per_blk, :nh], (0, 2, 1)).reshape(-1)
        qk_bound = dh ** 0.5 * jnp.max(jnp.abs(q_norm_gain[l])) * jnp.max(jnp.abs(k_norm_gain[l]))
        thr = (2.04 * LOG2E * qk_bound + 2.0).astype(F32).reshape(1)
        ob = lax.cond(thr[0] <= 2.0 * FIXED_OFFSET_MAX_LOGIT,
                      functools.partial(_attn, nh=nh, dh=dh, online=False),
                      functools.partial(_attn, nh=nh, dh=dh, online=True),
                      cb, thr, qb, kb, vb, crow, ccol)
        xp = _ffn(xp.reshape(bp * tp, d), oa.reshape(bp * tp, aw), ob.reshape(bp * tp, bw),
                  woa, wob, n2, wup, wdn).reshape(bp, tp, d)
        kp.append(k_p.reshape(bp, tp, nh, dh))
        vp.append(v_p.reshape(bp, tp, nh, dh))
        lp.append(lf_p)
        sp.append(s_p.astype(x_prompt.dtype))

        clf_pad = jnp.pad(cache_fox_logf[l].astype(F32), ((0, 0), (0, 0), (0, LANES - nh)))
        kb_c, vb_c, crow_c, c_tot = _cache_prep(
            cache_fox_k[l].astype(F32).reshape(bs, plen, bw), cache_fox_v[l].astype(F32).reshape(bs, plen, bw),
            clf_pad, nh=nh)
        (k_s, v_s, lf_s, qb, kb, vb, ccol, crow, _, _, oa, s_s) = mixer_in(
            xs, n1, w_fox, w_hgrn, bf_pad, qg2, kg2, lb_logits, gnorm, c_tot, state_hgrn[l].astype(F32), chunk=ts)
        ob = _sample_attn(qb, kb_c, vb_c, crow_c, kb, vb, crow, ccol, nh=nh, dh=dh)
        xs = _ffn(xs.reshape(bs * ts, d), oa.reshape(bs * ts, aw), ob.reshape(bs * ts, bw),
                  woa, wob, n2, wup, wdn).reshape(bs, ts, d)
        ksl.append(k_s.reshape(bs, ts, nh, dh))
        vsl.append(v_s.reshape(bs, ts, nh, dh))
        lsl.append(lf_s)
        ssl.append(s_s.astype(state_hgrn.dtype))

    return (xp, xs, jnp.stack(kp), jnp.stack(vp), jnp.stack(lp), jnp.stack(sp),
            jnp.stack(ksl), jnp.stack(vsl), jnp.stack(lsl), jnp.stack(ssl))
```

```python
import functools

import jax
import jax.numpy as jnp
from jax import lax
from jax.experimental import pallas as pl
from jax.experimental.pallas import tpu as pltpu

F32 = jnp.float32
BF16 = jnp.bfloat16

EPS = 1e-6
HGRN_CHUNK = 64
TOKEN_TILE = 512
ATTN_HEADS_PER_STEP = 8
LANES = 128
MXU_COLS = 256
NEG_BIG = -1e30
LOG2E = 1.4426950408889634
EXP2_ZERO_BELOW = -150.0
FIXED_OFFSET_MAX_LOGIT = 60.0
VMEM_LIMIT = 56 * 1024 * 1024

_NT = (((1,), (1,)), ((), ()))
_TN = (((0,), (0,)), ((), ()))


def _const_spec(shape):
    nd = len(shape)
    return pl.BlockSpec(shape, lambda *_: (0,) * nd, pipeline_mode=pl.Buffered(1))


def _cumsum_rows(x):
    n = x.shape[0]
    row = lax.broadcasted_iota(jnp.int32, x.shape, 0)
    s = 1
    while s < n:
        x = x + jnp.where(row >= s, pltpu.roll(x, s, axis=0), 0.0)
        s *= 2
    return x


def _pair_rms_norm(z, gain2, dh):
    lane = lax.broadcasted_iota(jnp.int32, z.shape, 1)
    first = lane < dh
    sq = z * z
    s0 = jnp.sum(jnp.where(first, sq, 0.0), axis=-1, keepdims=True)
    s1 = jnp.sum(jnp.where(first, 0.0, sq), axis=-1, keepdims=True)
    ms = jnp.where(first, s0, s1) * (1.0 / dh)
    return z * lax.rsqrt(ms + EPS) * gain2


def _log_sigmoid(x):
    return jnp.minimum(x, 0.0) - jnp.log1p(jnp.exp(-jnp.abs(x)))


def _hgrn_chunk(q, fa, ia, ga, lb, gn, st_ref, hd, causal):
    chunk = q.shape[0]
    f = lb + (1.0 - lb) * jax.nn.sigmoid(fa)
    k = 1.0 - f
    b = _cumsum_rows(jnp.log(f))
    b_ref = b[chunk // 2:chunk // 2 + 1, :]
    b_last = b[chunk - 1:chunk, :]
    v = ia.astype(BF16)
    st = st_ref[hd]
    inter = lax.dot_general((q * jnp.exp(b)).astype(BF16), st.astype(BF16), _NT, preferred_element_type=F32)
    qr = (q * jnp.exp(b - b_ref)).astype(BF16)
    kr = (k * jnp.exp(b_ref - b)).astype(BF16)
    a = lax.dot_general(qr, kr, _NT, preferred_element_type=F32)
    a = jnp.where(causal, a, 0.0).astype(BF16)
    o = inter + jnp.dot(a, v, preferred_element_type=F32)
    k3 = (k * jnp.exp(b_last - b)).astype(BF16)
    st_ref[hd] = st * jnp.exp(b_last) + lax.dot_general(v, k3, _TN, preferred_element_type=F32)
    ms = jnp.mean(o * o, axis=-1, keepdims=True)
    return (o * lax.rsqrt(ms + EPS) * gn * (ga * jax.nn.sigmoid(ga))).astype(BF16)


def _mixer_in_kernel(x_ref, n1_ref, wf_ref, wh_ref, bf_ref, qg_ref, kg_ref, lbl_ref, gn_ref, cin_ref, s0_ref,
                     k_out, v_out, lf_out, qb_out, kb_out, vb_out, ccol_out, crow_out, ctot_out, cends_out,
                     oa_out, s_out,
                     h_scr, zf_scr, zh_even, zh_odd, c_scr, lf_scr, carry_scr, st_scr,
                     *, layer, aw, bw, nh, dh, nha, rb, tiles_per_seq, n_tiles):
    t = pl.program_id(0)
    ip = jnp.minimum(t, n_tiles - 1) % tiles_per_seq
    ih = jnp.maximum(t - 1, 0) % tiles_per_seq

    @pl.when(t == 0)
    def _():
        zh_odd[...] = jnp.zeros(zh_odd.shape, F32)
        st_scr[...] = jnp.zeros(st_scr.shape, F32)

    @pl.when(jnp.logical_and(ip == 0, t < n_tiles))
    def _():
        carry_scr[...] = cin_ref[0]

    @pl.when(jnp.logical_and(ih == 0, t >= 1))
    def _():
        for hd in range(nha):
            st_scr[hd] = s0_ref[0, hd].T

    x = x_ref[0]
    ta = x.shape[0]
    nrb = ta // rb
    nsl = wh_ref.shape[0]
    ms = jnp.mean(x * x, axis=-1, keepdims=True)
    h_scr[...] = (x * lax.rsqrt(ms + EPS) * n1_ref[...]).astype(BF16)
    zf_scr[...] = jnp.dot(h_scr[...], wf_ref[...], preferred_element_type=F32)

    interleave = nrb == nsl

    lg = lbl_ref[...]
    e = jnp.exp(lg - jnp.max(lg, axis=0, keepdims=True))
    lb = jnp.sum(e[:layer + 1], axis=0, keepdims=True) / jnp.sum(e, axis=0, keepdims=True)

    scale = dh ** -0.5 * LOG2E
    rb_idx = lax.broadcasted_iota(jnp.int32, (nrb, LANES), 0)
    causal = (lax.broadcasted_iota(jnp.int32, (rb, rb), 0) >= lax.broadcasted_iota(jnp.int32, (rb, rb), 1))
    spw = aw // MXU_COLS
    hps = MXU_COLS // LANES

    def trip(zh_new, zh_old, r, state):
        carry, cends = state
        rows = pl.ds(pl.multiple_of(r * rb, rb), rb)
        if interleave:
            zh_new[r] = jnp.dot(h_scr[...], wh_ref[r], preferred_element_type=F32)

        lf = _log_sigmoid(zf_scr[rows, 3 * bw:3 * bw + LANES] + bf_ref[...])
        lf_scr[rows, :] = lf
        c = _cumsum_rows(lf) + carry
        c2 = c * LOG2E
        c_scr[rows, :] = c2
        ccol_out[0, rows, :] = c2
        for p in range(nh // 2):
            cols = slice(LANES * p, LANES * (p + 1))
            qn = _pair_rms_norm(zf_scr[rows, LANES * p:LANES * (p + 1)], qg_ref[...], dh) * scale
            kn = _pair_rms_norm(zf_scr[rows, bw + LANES * p:bw + LANES * (p + 1)], kg_ref[...], dh)
            vv = zf_scr[rows, 2 * bw + LANES * p:2 * bw + LANES * (p + 1)]
            qb_out[0, rows, cols] = qn.astype(BF16)
            kb_out[0, rows, cols] = kn.astype(BF16)
            vb_out[0, rows, cols] = vv.astype(BF16)
            k_out[0, rows, cols] = kn
            v_out[0, rows, cols] = vv

        for hd in range(nha):
            sl, ln = hd // hps, slice(LANES * (hd % hps), LANES * (hd % hps + 1))
            cols = slice(LANES * hd, LANES * (hd + 1))
            oa_out[0, rows, cols] = _hgrn_chunk(
                zh_old[sl, rows, ln], zh_old[spw + sl, rows, ln],
                zh_old[2 * spw + sl, rows, ln], zh_old[3 * spw + sl, rows, ln],
                lb[:, cols], gn_ref[:, cols], st_scr, hd, causal)

        carry = c[rb - 1:rb, :]
        return carry, jnp.where(rb_idx == r, carry, cends)

    def run(zh_new, zh_old):
        if not interleave:
            for sl in range(nsl):
                zh_new[sl] = jnp.dot(h_scr[...], wh_ref[sl], preferred_element_type=F32)
        carry, cends = lax.fori_loop(0, nrb, functools.partial(trip, zh_new, zh_old),
                                     (carry_scr[...], jnp.zeros((nrb, LANES), F32)),
                                     unroll=max(u for u in (4, 2, 1) if nrb % u == 0))
        ctot_out[0] = carry
        cends_out[0] = cends

        @pl.when(t < n_tiles - 1)
        def _():
            carry_scr[...] = carry

    pl.when(t % 2 == 0)(lambda: run(zh_even, zh_odd))
    pl.when(t % 2 == 1)(lambda: run(zh_odd, zh_even))
    crow_out[0, 0] = c_scr[...].T[:nh, :]
    lf_out[0] = lf_scr[...].T[:nh, :]

    @pl.when(ih == tiles_per_seq - 1)
    def _():
        for hd in range(nha):
            s_out[0, hd] = st_scr[hd].T


def _mixer_in(x, n1, w_fox, w_hgrn, bf_pad, qg2, kg2, lb_logits, gnorm, carry_in, s0, *, layer, nh, dh, chunk):
    b, t, d = x.shape
    nha, dk, dv = s0.shape[1], s0.shape[2], s0.shape[3]
    aw, bw = nha * dk, nh * dh
    ta = min(TOKEN_TILE, t)
    rb = chunk
    tps = t // ta
    n_tiles = b * tps
    assert ta % rb == 0 and t % ta == 0

    def proj_side(*tail):
        return lambda s: (jnp.minimum(s, n_tiles - 1) // tps, jnp.minimum(s, n_tiles - 1) % tps) + tail

    def rec_side(*tail):
        return lambda s: (jnp.maximum(s - 1, 0) // tps, jnp.maximum(s - 1, 0) % tps) + tail

    proj_seq = lambda s: (jnp.minimum(s, n_tiles - 1) // tps, 0, 0)
    rec_seq = lambda s: (jnp.maximum(s - 1, 0) // tps, 0, 0, 0)
    tok = proj_side(0)
    out_shape = (
        jax.ShapeDtypeStruct((b, t, bw), F32),
        jax.ShapeDtypeStruct((b, t, bw), F32),
        jax.ShapeDtypeStruct((b, nh, t), F32),
        jax.ShapeDtypeStruct((b, t, bw), BF16),
        jax.ShapeDtypeStruct((b, t, bw), BF16),
        jax.ShapeDtypeStruct((b, t, bw), BF16),
        jax.ShapeDtypeStruct((b, t, LANES), F32),
        jax.ShapeDtypeStruct((b, tps, nh, ta), F32),
        jax.ShapeDtypeStruct((b, 1, LANES), F32),
        jax.ShapeDtypeStruct((b, t // rb, LANES), F32),
        jax.ShapeDtypeStruct((b, t, aw), BF16),
        jax.ShapeDtypeStruct((b, nha, dk, dv), F32),
    )
    out_specs = (
        pl.BlockSpec((1, ta, bw), tok), pl.BlockSpec((1, ta, bw), tok),
        pl.BlockSpec((1, nh, ta), lambda s: (jnp.minimum(s, n_tiles - 1) // tps, 0, jnp.minimum(s, n_tiles - 1) % tps)),
        pl.BlockSpec((1, ta, bw), tok), pl.BlockSpec((1, ta, bw), tok), pl.BlockSpec((1, ta, bw), tok),
        pl.BlockSpec((1, ta, LANES), tok),
        pl.BlockSpec((1, 1, nh, ta), proj_side(0, 0)),
        pl.BlockSpec((1, 1, LANES), proj_seq),
        pl.BlockSpec((1, ta // rb, LANES), tok),
        pl.BlockSpec((1, ta, aw), rec_side(0)),
        pl.BlockSpec((1, nha, dk, dv), rec_seq),
    )
    in_specs = [
        pl.BlockSpec((1, ta, d), tok),
        _const_spec(n1.shape), _const_spec(w_fox.shape), _const_spec(w_hgrn.shape), _const_spec(bf_pad.shape),
        _const_spec(qg2.shape), _const_spec(kg2.shape), _const_spec(lb_logits.shape), _const_spec(gnorm.shape),
        pl.BlockSpec((1, 1, LANES), proj_seq),
        pl.BlockSpec((1, nha, dk, dv), rec_seq),
    ]
    nsl = w_hgrn.shape[0]
    return pl.pallas_call(
        functools.partial(_mixer_in_kernel, layer=layer, aw=aw, bw=bw, nh=nh, dh=dh, nha=nha, rb=rb,
                          tiles_per_seq=tps, n_tiles=n_tiles),
        out_shape=out_shape,
        grid=(n_tiles + 1,),
        in_specs=in_specs,
        out_specs=out_specs,
        scratch_shapes=[pltpu.VMEM((ta, d), BF16), pltpu.VMEM((ta, w_fox.shape[1]), F32),
                        pltpu.VMEM((nsl, ta, MXU_COLS), F32), pltpu.VMEM((nsl, ta, MXU_COLS), F32),
                        pltpu.VMEM((ta, LANES), F32), pltpu.VMEM((ta, LANES), F32),
                        pltpu.VMEM((1, LANES), F32), pltpu.VMEM((nha, dv, dk), F32)],
        compiler_params=pltpu.CompilerParams(dimension_semantics=("arbitrary",), vmem_limit_bytes=VMEM_LIMIT),
        name="mixer_in",
    )(x, n1, w_fox, w_hgrn, bf_pad, qg2, kg2, lb_logits, gnorm, carry_in, s0)


def _one_head(q_pair, odd, dh):
    lane = lax.broadcasted_iota(jnp.int32, q_pair.shape, 1)
    keep = lane >= dh if odd else lane < dh
    return jnp.where(keep, q_pair, jnp.zeros_like(q_pair))


def _head_column(c_block, lane_idx):
    lane = lax.broadcasted_iota(jnp.int32, c_block.shape, 1)
    col = jnp.sum(jnp.where(lane == lane_idx, c_block, 0.0), axis=-1, keepdims=True)
    return jnp.broadcast_to(col, c_block.shape)


def _flash_step(q, k, v, ck, cq, m_ref, l_ref, acc_ref, hh, rows, mask):
    t = lax.dot_general(q, k, _NT, preferred_element_type=F32) - ck
    if mask is not None:
        t = jnp.where(mask, t, NEG_BIG)
    m_prev = m_ref[hh, rows, :]
    m_new = jnp.maximum(m_prev, jnp.max(t, axis=-1, keepdims=True) + cq)
    alpha = jnp.exp2(m_prev - m_new)
    off = m_new - cq
    ps = [jnp.exp2(t[:, LANES * c:LANES * (c + 1)] - off) for c in range(t.shape[1] // LANES)]
    l_ref[hh, rows, :] = alpha * l_ref[hh, rows, :] + sum(ps[1:], ps[0])
    p = jnp.concatenate(ps, axis=1).astype(BF16)
    acc_ref[hh, rows, :] = alpha * acc_ref[hh, rows, :] + jnp.dot(p, v, preferred_element_type=F32)
    m_ref[hh, rows, :] = m_new


def _fixed_offset_step(q, k, v, ck, cq, l_ref, acc_ref, hh, rows, mask):
    s = lax.dot_general(q, k, _NT, preferred_element_type=F32)
    ps = []
    for c in range(s.shape[1] // LANES):
        cols = slice(LANES * c, LANES * (c + 1))
        t = s[:, cols] - ck[:, cols] + cq
        if mask is not None:
            t = jnp.where(mask[:, cols], t, NEG_BIG)
        ps.append(jnp.exp2(t))
    l_ref[hh, rows, :] += sum(ps[1:], ps[0])
    p = jnp.concatenate(ps, axis=1).astype(BF16)
    acc_ref[hh, rows, :] += jnp.dot(p, v, preferred_element_type=F32)


def _merge_pair(acc0, l0, acc1, l1, dh):
    o0 = acc0 / jnp.sum(l0, axis=-1, keepdims=True)
    o1 = acc1 / jnp.sum(l1, axis=-1, keepdims=True)
    lane = lax.broadcasted_iota(jnp.int32, o0.shape, 1)
    return jnp.where(lane < dh, o0, o1)


def _attn_kernel(cb_ref, thr_ref, q_ref, k_ref, v_ref, crow_ref, ccol_ref, o_ref, m_scr, l_scr, acc_scr,
                 *, tq, dh, nh, hps, online):
    bi, g, i = pl.program_id(0), pl.program_id(1), pl.program_id(2)
    nblk = pl.num_programs(2)
    acc_scr[...] = jnp.zeros(acc_scr.shape, F32)
    l_scr[...] = jnp.zeros(l_scr.shape, F32)
    if online:
        m_scr[...] = jnp.full(m_scr.shape, NEG_BIG, F32)

    def step(q, k, v, ck, cq, hh, rows, mask):
        if online:
            _flash_step(q, k, v, ck, cq, m_scr, l_scr, acc_scr, hh, rows, mask)
        else:
            _fixed_offset_step(q, k, v, ck, cq, l_scr, acc_scr, hh, rows, mask)

    def first_live_block(hd):
        base = (bi * nh + hd) * nblk
        c_before = cb_ref[base + jnp.maximum(i - 1, 0)]

        def previous_is_live(n):
            c_end = cb_ref[base + jnp.maximum(n - 1, 0)]
            return jnp.logical_and(n > 0, (c_before - c_end) * LOG2E + thr_ref[0] >= EXP2_ZERO_BELOW)

        return lax.while_loop(previous_is_live, lambda n: n - 1, i)

    j_lo = first_live_block(hps * g)
    for hh in range(1, hps):
        j_lo = jnp.minimum(j_lo, first_live_block(hps * g + hh))

    def q_of(hh, rows=slice(None)):
        return _one_head(q_ref[0, rows, LANES * (hh // 2):LANES * (hh // 2 + 1)], hh % 2, dh)

    def cq_of(hh, rows=slice(None)):
        return _head_column(ccol_ref[0, rows, :], hps * g + hh)

    def body(j, _):
        keys = pl.ds(pl.multiple_of(j * tq, tq), tq)
        for hh in range(hps):
            kcols = slice(LANES * (hh // 2), LANES * (hh // 2 + 1))
            step(q_of(hh), k_ref[0, keys, kcols], v_ref[0, keys, kcols], crow_ref[0, j, 0, hh:hh + 1, :],
                 cq_of(hh), hh, slice(0, tq), None)
        return 0

    lax.fori_loop(j_lo, i, body, 0)

    half = tq // 2
    keys_a = pl.ds(pl.multiple_of(i * tq, tq), half)
    keys_b = pl.ds(pl.multiple_of(i * tq + half, half), half)
    mask_a = (lax.broadcasted_iota(jnp.int32, (tq, half), 0) >= lax.broadcasted_iota(jnp.int32, (tq, half), 1))
    mask_b = (lax.broadcasted_iota(jnp.int32, (half, half), 0) >= lax.broadcasted_iota(jnp.int32, (half, half), 1))
    late = slice(half, tq)
    for hh in range(hps):
        kcols = slice(LANES * (hh // 2), LANES * (hh // 2 + 1))
        step(q_of(hh), k_ref[0, keys_a, kcols], v_ref[0, keys_a, kcols], crow_ref[0, i, 0, hh:hh + 1, :half],
             cq_of(hh), hh, slice(0, tq), mask_a)
        step(q_of(hh, late), k_ref[0, keys_b, kcols], v_ref[0, keys_b, kcols],
             crow_ref[0, i, 0, hh:hh + 1, half:], cq_of(hh, late), hh, late, mask_b)
    for p in range(hps // 2):
        o_ref[0, :, LANES * p:LANES * (p + 1)] = _merge_pair(
            acc_scr[2 * p], l_scr[2 * p], acc_scr[2 * p + 1], l_scr[2 * p + 1], dh).astype(o_ref.dtype)


def _attn(cb, thr, qb, kb, vb, crow, ccol, *, nh, dh, online):
    b, t, _ = qb.shape
    tq = min(TOKEN_TILE, t)
    hps = min(ATTN_HEADS_PER_STEP, nh)
    ng = nh // hps
    crow5 = crow.reshape(b, t // tq, ng, hps, tq)
    grid_spec = pltpu.PrefetchScalarGridSpec(
        num_scalar_prefetch=2,
        grid=(b, ng, t // tq),
        in_specs=[pl.BlockSpec((1, tq, hps * dh), lambda bi, g, i, *_: (bi, i, g)),
                  pl.BlockSpec((1, t, hps * dh), lambda bi, g, i, *_: (bi, 0, g)),
                  pl.BlockSpec((1, t, hps * dh), lambda bi, g, i, *_: (bi, 0, g)),
                  pl.BlockSpec((1, t // tq, 1, hps, tq), lambda bi, g, i, *_: (bi, 0, g, 0, 0)),
                  pl.BlockSpec((1, tq, LANES), lambda bi, g, i, *_: (bi, i, 0))],
        out_specs=pl.BlockSpec((1, tq, hps * dh), lambda bi, g, i, *_: (bi, i, g)),
        scratch_shapes=[pltpu.VMEM((hps, tq, LANES), F32), pltpu.VMEM((hps, tq, LANES), F32),
                        pltpu.VMEM((hps, tq, LANES), F32)],
    )
    return pl.pallas_call(
        functools.partial(_attn_kernel, tq=tq, dh=dh, nh=nh, hps=hps, online=online),
        out_shape=jax.ShapeDtypeStruct((b, t, nh * dh), BF16),
        grid_spec=grid_spec,
        compiler_params=pltpu.CompilerParams(
            dimension_semantics=("arbitrary", "arbitrary", "arbitrary"), vmem_limit_bytes=VMEM_LIMIT),
        name="fox_attn_online" if online else "fox_attn",
    )(cb, thr, qb, kb, vb, crow5, ccol)


def _sample_attn_kernel(q_ref, kn_ref, vn_ref, crn_ref, ccol_ref, kc_ref, vc_ref, clf_ref, o_ref, c_scr,
                        *, nh, dh, rb):
    s_len, p_len = q_ref.shape[1], kc_ref.shape[1]

    def cache_rows(r, carry):
        rows = pl.ds(pl.multiple_of(r * rb, rb), rb)
        c = _cumsum_rows(clf_ref[0, rows, :]) + carry
        c_scr[rows, :] = c
        return c[rb - 1:rb, :]

    c_end = lax.fori_loop(0, p_len // rb, cache_rows, jnp.zeros((1, LANES), F32))
    c_cache = ((c_scr[...] - c_end) * LOG2E).T[:nh, :]

    causal = (lax.broadcasted_iota(jnp.int32, (s_len, s_len), 0)
              >= lax.broadcasted_iota(jnp.int32, (s_len, s_len), 1))
    for hd in range(nh):
        cols = slice(dh * hd, dh * (hd + 1))
        q = q_ref[0, :, cols]
        cq = ccol_ref[0, :, hd:hd + 1]
        kc = kc_ref[0, :, hd, :].astype(BF16)
        vc = vc_ref[0, :, hd, :].astype(BF16)
        t1 = lax.dot_general(q, kc, _NT, preferred_element_type=F32) - c_cache[hd:hd + 1, :] + cq
        t2 = lax.dot_general(q, kn_ref[0, :, cols], _NT, preferred_element_type=F32) - crn_ref[0, 0, hd:hd + 1, :] + cq
        t2 = jnp.where(causal, t2, NEG_BIG)
        m = jnp.maximum(jnp.max(t1, axis=-1, keepdims=True), jnp.max(t2, axis=-1, keepdims=True))
        p1 = jnp.exp2(t1 - m)
        p2 = jnp.exp2(t2 - m)
        den = jnp.sum(p1, axis=-1, keepdims=True) + jnp.sum(p2, axis=-1, keepdims=True)
        acc = (jnp.dot(p1.astype(BF16), vc, preferred_element_type=F32)
               + jnp.dot(p2.astype(BF16), vn_ref[0, :, cols], preferred_element_type=F32))
        o_ref[0, :, cols] = (acc / den).astype(o_ref.dtype)


def _sample_attn(qb, kb_n, vb_n, crow_n, ccol, cache_k, cache_v, clf_pad, *, nh, dh):
    b, s_len, bw = qb.shape
    p_len = cache_k.shape[1]
    whole = lambda bi: (bi, 0, 0)
    whole4 = lambda bi: (bi, 0, 0, 0)
    return pl.pallas_call(
        functools.partial(_sample_attn_kernel, nh=nh, dh=dh, rb=min(HGRN_CHUNK, p_len)),
        out_shape=jax.ShapeDtypeStruct((b, s_len, bw), BF16),
        grid=(b,),
        in_specs=[pl.BlockSpec((1, s_len, bw), whole), pl.BlockSpec((1, s_len, bw), whole),
                  pl.BlockSpec((1, s_len, bw), whole), pl.BlockSpec((1, 1, nh, s_len), whole4),
                  pl.BlockSpec((1, s_len, LANES), whole),
                  pl.BlockSpec((1, p_len, nh, dh), whole4), pl.BlockSpec((1, p_len, nh, dh), whole4),
                  pl.BlockSpec((1, p_len, LANES), whole)],
        out_specs=pl.BlockSpec((1, s_len, bw), whole),
        scratch_shapes=[pltpu.VMEM((p_len, LANES), F32)],
        compiler_params=pltpu.CompilerParams(dimension_semantics=("arbitrary",), vmem_limit_bytes=VMEM_LIMIT),
        name="fox_sample_attn",
    )(qb, kb_n, vb_n, crow_n, ccol, cache_k, cache_v, clf_pad)


def _ffn_kernel(x_ref, oa_ref, ob_ref, woa_ref, wob_ref, n2_ref, wup_ref, wdn_ref, y_ref, h_scr, *, ff_chunk):
    x1 = (x_ref[...]
          + jnp.dot(oa_ref[...], woa_ref[...], preferred_element_type=F32)
          + jnp.dot(ob_ref[...], wob_ref[...], preferred_element_type=F32))
    ms = jnp.mean(x1 * x1, axis=-1, keepdims=True)
    h_scr[...] = (x1 * lax.rsqrt(ms + EPS) * n2_ref[...]).astype(BF16)
    y_ref[...] = x1
    for c in range(wup_ref.shape[1] // ff_chunk):
        cs = slice(ff_chunk * c, ff_chunk * (c + 1))
        u = jnp.maximum(jnp.dot(h_scr[...], wup_ref[:, cs], preferred_element_type=F32), 0.0)
        y_ref[...] += jnp.dot((u * u).astype(BF16), wdn_ref[cs, :], preferred_element_type=F32)


def _ffn(x2d, oa2d, ob2d, woa, wob, n2, wup, wdn):
    n, d = x2d.shape
    tc = min(TOKEN_TILE, n)
    aw, bw = oa2d.shape[1], ob2d.shape[1]
    row = lambda i: (i, 0)
    return pl.pallas_call(
        functools.partial(_ffn_kernel, ff_chunk=min(1024, wup.shape[1])),
        out_shape=jax.ShapeDtypeStruct((n, d), F32),
        grid=(n // tc,),
        in_specs=[pl.BlockSpec((tc, d), row), pl.BlockSpec((tc, aw), row), pl.BlockSpec((tc, bw), row),
                  _const_spec(woa.shape), _const_spec(wob.shape), _const_spec(n2.shape),
                  _const_spec(wup.shape), _const_spec(wdn.shape)],
        out_specs=pl.BlockSpec((tc, d), row),
        scratch_shapes=[pltpu.VMEM((tc, d), BF16)],
        compiler_params=pltpu.CompilerParams(
            dimension_semantics=("arbitrary",), vmem_limit_bytes=VMEM_LIMIT),
        name="out_ffn",
    )(x2d, oa2d, ob2d, woa, wob, n2, wup, wdn)


def kernel(x_prompt, x_sample, cache_fox_k, cache_fox_v, cache_fox_logf, state_hgrn, norm1, w_in, b_fox_f,
           q_norm_gain, k_norm_gain, hgrn_lb_logits, hgrn_out_norm, w_out, norm2, w_up, w_down):
    depth = w_in.shape[0]
    d = x_prompt.shape[-1]
    nh, dh = cache_fox_k.shape[3], cache_fox_k.shape[4]
    nha, dk, dv = state_hgrn.shape[2], state_hgrn.shape[3], state_hgrn.shape[4]
    aw, bw = nha * dk, nh * dh
    assert dk == LANES and dv == LANES and 2 * dh == LANES and nh % 2 == 0 and nh <= 8
    assert w_in.shape[2] == 4 * aw + 3 * bw + nh and aw + bw == w_out.shape[1] and aw % MXU_COLS == 0

    xp, xs = x_prompt, x_sample
    bp, tp, _ = xp.shape
    bs, ts, _ = xs.shape
    plen = cache_fox_k.shape[2]
    lb_logits = hgrn_lb_logits.astype(F32)

    kp, vp, lp, sp = [], [], [], []
    ksl, vsl, lsl, ssl = [], [], [], []
    for l in range(depth):
        w_fox = jnp.pad(w_in[l, :, 4 * aw:], ((0, 0), (0, LANES - nh))).astype(BF16)
        w_hgrn = jnp.transpose(w_in[l, :, :4 * aw].reshape(d, 4 * aw // MXU_COLS, MXU_COLS), (1, 0, 2)).astype(BF16)
        bf_pad = jnp.pad(b_fox_f[l].astype(F32), (0, LANES - nh)).reshape(1, LANES)
        qg2 = jnp.tile(q_norm_gain[l].astype(F32), 2).reshape(1, LANES)
        kg2 = jnp.tile(k_norm_gain[l].astype(F32), 2).reshape(1, LANES)
        n1 = norm1[l].astype(F32).reshape(1, d)
        n2 = norm2[l].astype(F32).reshape(1, d)
        gnorm = hgrn_out_norm[l].astype(F32).reshape(1, aw)
        woa = w_out[l, :aw].astype(BF16)
        wob = w_out[l, aw:].astype(BF16)
        wup = w_up[l].astype(BF16)
        wdn = w_down[l].astype(BF16)
        mixer_in = functools.partial(_mixer_in, layer=l, nh=nh, dh=dh)

        (k_p, v_p, lf_p, qb, kb, vb, ccol, crow, _, cends, oa, s_p) = mixer_in(
            xp, n1, w_fox, w_hgrn, bf_pad, qg2, kg2, lb_logits, gnorm, jnp.zeros((bp, 1, LANES), F32),
            jnp.zeros((bp, nha, dk, dv), F32), chunk=min(HGRN_CHUNK, tp))
        per_blk = cends.shape[1] // (tp // min(TOKEN_TILE, tp))
        cb = jnp.transpose(cends[:, per_blk - 1::per_blk, :nh], (0, 2, 1)).reshape(-1)
        qk_bound = dh ** 0.5 * jnp.max(jnp.abs(q_norm_gain[l])) * jnp.max(jnp.abs(k_norm_gain[l]))
        thr = (2.04 * LOG2E * qk_bound + 2.0).astype(F32).reshape(1)
        ob = lax.cond(thr[0] <= 2.0 * FIXED_OFFSET_MAX_LOGIT,
                      functools.partial(_attn, nh=nh, dh=dh, online=False),
                      functools.partial(_attn, nh=nh, dh=dh, online=True),
                      cb, thr, qb, kb, vb, crow, ccol)
        xp = _ffn(xp.reshape(bp * tp, d), oa.reshape(bp * tp, aw), ob.reshape(bp * tp, bw),
                  woa, wob, n2, wup, wdn).reshape(bp, tp, d)
        kp.append(k_p.reshape(bp, tp, nh, dh))
        vp.append(v_p.reshape(bp, tp, nh, dh))
        lp.append(jnp.transpose(lf_p, (0, 2, 1)))
        sp.append(s_p.astype(x_prompt.dtype))

        clf_pad = jnp.pad(cache_fox_logf[l].astype(F32), ((0, 0), (0, 0), (0, LANES - nh)))
        (k_s, v_s, lf_s, qb, kb, vb, ccol, crow, _, _, oa, s_s) = mixer_in(
            xs, n1, w_fox, w_hgrn, bf_pad, qg2, kg2, lb_logits, gnorm, jnp.zeros((bs, 1, LANES), F32),
            state_hgrn[l].astype(F32), chunk=ts)
        ob = _sample_attn(qb, kb, vb, crow, ccol, cache_fox_k[l].astype(F32), cache_fox_v[l].astype(F32),
                          clf_pad, nh=nh, dh=dh)
        xs = _ffn(xs.reshape(bs * ts, d), oa.reshape(bs * ts, aw), ob.reshape(bs * ts, bw),
                  woa, wob, n2, wup, wdn).reshape(bs, ts, d)
        ksl.append(k_s.reshape(bs, ts, nh, dh))
        vsl.append(v_s.reshape(bs, ts, nh, dh))
        lsl.append(jnp.transpose(lf_s, (0, 2, 1)))
        ssl.append(s_s.astype(state_hgrn.dtype))

    return (xp, xs, jnp.stack(kp), jnp.stack(vp), jnp.stack(lp), jnp.stack(sp),
            jnp.stack(ksl), jnp.stack(vsl), jnp.stack(lsl), jnp.stack(ssl))
```

```python
import functools

import jax
import jax.numpy as jnp
from jax import lax
from jax.experimental import pallas as pl
from jax.experimental.pallas import tpu as pltpu

F32 = jnp.float32
BF16 = jnp.bfloat16

EPS = 1e-6
HGRN_CHUNK = 64
TOKEN_TILE = 512
ATTN_HEADS_PER_STEP = 8
LANES = 128
MXU_COLS = 256
NEG_BIG = -1e30
LOG2E = 1.4426950408889634
EXP2_ZERO_BELOW = -150.0
FIXED_OFFSET_MAX_LOGIT = 60.0
VMEM_LIMIT = 56 * 1024 * 1024

_NT = (((1,), (1,)), ((), ()))
_TN = (((0,), (0,)), ((), ()))


def _const_spec(shape):
    nd = len(shape)
    return pl.BlockSpec(shape, lambda *_: (0,) * nd, pipeline_mode=pl.Buffered(1))


def _cumsum_rows(x):
    n = x.shape[0]
    row = lax.broadcasted_iota(jnp.int32, x.shape, 0)
    s = 1
    while s < n:
        x = x + jnp.where(row >= s, pltpu.roll(x, s, axis=0), 0.0)
        s *= 2
    return x


def _pair_rms_norm(z, gain2, dh):
    lane = lax.broadcasted_iota(jnp.int32, z.shape, 1)
    first = lane < dh
    sq = z * z
    s0 = jnp.sum(jnp.where(first, sq, 0.0), axis=-1, keepdims=True)
    s1 = jnp.sum(jnp.where(first, 0.0, sq), axis=-1, keepdims=True)
    ms = jnp.where(first, s0, s1) * (1.0 / dh)
    return z * lax.rsqrt(ms + EPS) * gain2


def _log_sigmoid(x):
    return jnp.minimum(x, 0.0) - jnp.log(1.0 + jnp.exp(-jnp.abs(x)))


def _hgrn_chunk(q, fa, ia, ga, lb, gn, st_ref, hd, causal):
    chunk = q.shape[0]
    f = lb + (1.0 - lb) * jax.nn.sigmoid(fa)
    k = 1.0 - f
    b = _cumsum_rows(jnp.log(f))
    b_ref = b[chunk // 2:chunk // 2 + 1, :]
    b_last = b[chunk - 1:chunk, :]
    v = ia.astype(BF16)
    st = st_ref[hd]
    inter = lax.dot_general((q * jnp.exp(b)).astype(BF16), st.astype(BF16), _NT, preferred_element_type=F32)
    qr = (q * jnp.exp(b - b_ref)).astype(BF16)
    kr = (k * jnp.exp(b_ref - b)).astype(BF16)
    a = lax.dot_general(qr, kr, _NT, preferred_element_type=F32)
    a = jnp.where(causal, a, 0.0).astype(BF16)
    o = inter + jnp.dot(a, v, preferred_element_type=F32)
    k3 = (k * jnp.exp(b_last - b)).astype(BF16)
    st_ref[hd] = st * jnp.exp(b_last) + lax.dot_general(v, k3, _TN, preferred_element_type=F32)
    ms = jnp.mean(o * o, axis=-1, keepdims=True)
    return (o * lax.rsqrt(ms + EPS) * gn * (ga * jax.nn.sigmoid(ga))).astype(BF16)


def _mixer_in_kernel(x_ref, n1_ref, wf_ref, wh_ref, bf_ref, qg_ref, kg_ref, lbl_ref, gn_ref, cin_ref, s0_ref,
                     k_out, v_out, lf_out, qb_out, kb_out, vb_out, ccol_out, crow_out, ctot_out, cends_out,
                     oa_out, s_out,
                     h_scr, zf_scr, zh_even, zh_odd, c_scr, lf_scr, carry_scr, st_scr,
                     *, layer, aw, bw, nh, dh, nha, rb, tiles_per_seq, n_tiles):
    t = pl.program_id(0)
    ip = jnp.minimum(t, n_tiles - 1) % tiles_per_seq
    ih = jnp.maximum(t - 1, 0) % tiles_per_seq

    @pl.when(t == 0)
    def _():
        zh_odd[...] = jnp.zeros(zh_odd.shape, F32)
        st_scr[...] = jnp.zeros(st_scr.shape, F32)

    @pl.when(jnp.logical_and(ip == 0, t < n_tiles))
    def _():
        carry_scr[...] = cin_ref[0]

    @pl.when(jnp.logical_and(ih == 0, t >= 1))
    def _():
        for hd in range(nha):
            st_scr[hd] = s0_ref[0, hd].T

    x = x_ref[0]
    ta = x.shape[0]
    nrb = ta // rb
    nsl = wh_ref.shape[0]
    ms = jnp.mean(x * x, axis=-1, keepdims=True)
    h_scr[...] = (x * lax.rsqrt(ms + EPS) * n1_ref[...]).astype(BF16)
    zf_scr[...] = jnp.dot(h_scr[...], wf_ref[...], preferred_element_type=F32)

    interleave = nrb == nsl

    lg = lbl_ref[...]
    e = jnp.exp(lg - jnp.max(lg, axis=0, keepdims=True))
    lb = jnp.sum(e[:layer + 1], axis=0, keepdims=True) / jnp.sum(e, axis=0, keepdims=True)

    scale = dh ** -0.5 * LOG2E
    rb_idx = lax.broadcasted_iota(jnp.int32, (nrb, LANES), 0)
    causal = (lax.broadcasted_iota(jnp.int32, (rb, rb), 0) >= lax.broadcasted_iota(jnp.int32, (rb, rb), 1))
    spw = aw // MXU_COLS
    hps = MXU_COLS // LANES

    def trip(zh_new, zh_old, r, state):
        carry, cends = state
        rows = pl.ds(pl.multiple_of(r * rb, rb), rb)
        if interleave:
            zh_new[r] = jnp.dot(h_scr[...], wh_ref[r], preferred_element_type=F32)

        lf = _log_sigmoid(zf_scr[rows, 3 * bw:3 * bw + LANES] + bf_ref[...])
        lf_scr[rows, :] = lf
        c = _cumsum_rows(lf) + carry
        c2 = c * LOG2E
        c_scr[rows, :] = c2
        ccol_out[0, rows, :] = c2
        for p in range(nh // 2):
            cols = slice(LANES * p, LANES * (p + 1))
            qn = _pair_rms_norm(zf_scr[rows, LANES * p:LANES * (p + 1)], qg_ref[...], dh) * scale
            kn = _pair_rms_norm(zf_scr[rows, bw + LANES * p:bw + LANES * (p + 1)], kg_ref[...], dh)
            vv = zf_scr[rows, 2 * bw + LANES * p:2 * bw + LANES * (p + 1)]
            qb_out[0, rows, cols] = qn.astype(BF16)
            kb_out[0, rows, cols] = kn.astype(BF16)
            vb_out[0, rows, cols] = vv.astype(BF16)
            k_out[0, rows, cols] = kn
            v_out[0, rows, cols] = vv

        for hd in range(nha):
            sl, ln = hd // hps, slice(LANES * (hd % hps), LANES * (hd % hps + 1))
            cols = slice(LANES * hd, LANES * (hd + 1))
            oa_out[0, rows, cols] = _hgrn_chunk(
                zh_old[sl, rows, ln], zh_old[spw + sl, rows, ln],
                zh_old[2 * spw + sl, rows, ln], zh_old[3 * spw + sl, rows, ln],
                lb[:, cols], gn_ref[:, cols], st_scr, hd, causal)

        carry = c[rb - 1:rb, :]
        return carry, jnp.where(rb_idx == r, carry, cends)

    def run(zh_new, zh_old):
        if not interleave:
            for sl in range(nsl):
                zh_new[sl] = jnp.dot(h_scr[...], wh_ref[sl], preferred_element_type=F32)
        carry, cends = lax.fori_loop(0, nrb, functools.partial(trip, zh_new, zh_old),
                                     (carry_scr[...], jnp.zeros((nrb, LANES), F32)),
                                     unroll=max(u for u in (8, 4, 2, 1) if nrb % u == 0))
        ctot_out[0] = carry
        cends_out[0] = cends

        @pl.when(t < n_tiles - 1)
        def _():
            carry_scr[...] = carry

    pl.when(t % 2 == 0)(lambda: run(zh_even, zh_odd))
    pl.when(t % 2 == 1)(lambda: run(zh_odd, zh_even))
    crow_out[0, 0] = c_scr[...].T[:nh, :]
    lf_out[0] = lf_scr[...].T[:nh, :]

    @pl.when(ih == tiles_per_seq - 1)
    def _():
        for hd in range(nha):
            s_out[0, hd] = st_scr[hd].T


def _mixer_in(x, n1, w_fox, w_hgrn, bf_pad, qg2, kg2, lb_logits, gnorm, carry_in, s0, *, layer, nh, dh, chunk):
    b, t, d = x.shape
    nha, dk, dv = s0.shape[1], s0.shape[2], s0.shape[3]
    aw, bw = nha * dk, nh * dh
    ta = min(TOKEN_TILE, t)
    rb = chunk
    tps = t // ta
    n_tiles = b * tps
    assert ta % rb == 0 and t % ta == 0

    def proj_side(*tail):
        return lambda s: (jnp.minimum(s, n_tiles - 1) // tps, jnp.minimum(s, n_tiles - 1) % tps) + tail

    def rec_side(*tail):
        return lambda s: (jnp.maximum(s - 1, 0) // tps, jnp.maximum(s - 1, 0) % tps) + tail

    proj_seq = lambda s: (jnp.minimum(s, n_tiles - 1) // tps, 0, 0)
    rec_seq = lambda s: (jnp.maximum(s - 1, 0) // tps, 0, 0, 0)
    tok = proj_side(0)
    out_shape = (
        jax.ShapeDtypeStruct((b, t, bw), F32),
        jax.ShapeDtypeStruct((b, t, bw), F32),
        jax.ShapeDtypeStruct((b, nh, t), F32),
        jax.ShapeDtypeStruct((b, t, bw), BF16),
        jax.ShapeDtypeStruct((b, t, bw), BF16),
        jax.ShapeDtypeStruct((b, t, bw), BF16),
        jax.ShapeDtypeStruct((b, t, LANES), F32),
        jax.ShapeDtypeStruct((b, tps, nh, ta), F32),
        jax.ShapeDtypeStruct((b, 1, LANES), F32),
        jax.ShapeDtypeStruct((b, t // rb, LANES), F32),
        jax.ShapeDtypeStruct((b, t, aw), BF16),
        jax.ShapeDtypeStruct((b, nha, dk, dv), F32),
    )
    out_specs = (
        pl.BlockSpec((1, ta, bw), tok), pl.BlockSpec((1, ta, bw), tok),
        pl.BlockSpec((1, nh, ta), lambda s: (jnp.minimum(s, n_tiles - 1) // tps, 0, jnp.minimum(s, n_tiles - 1) % tps)),
        pl.BlockSpec((1, ta, bw), tok), pl.BlockSpec((1, ta, bw), tok), pl.BlockSpec((1, ta, bw), tok),
        pl.BlockSpec((1, ta, LANES), tok),
        pl.BlockSpec((1, 1, nh, ta), proj_side(0, 0)),
        pl.BlockSpec((1, 1, LANES), proj_seq),
        pl.BlockSpec((1, ta // rb, LANES), tok),
        pl.BlockSpec((1, ta, aw), rec_side(0)),
        pl.BlockSpec((1, nha, dk, dv), rec_seq),
    )
    in_specs = [
        pl.BlockSpec((1, ta, d), tok),
        _const_spec(n1.shape), _const_spec(w_fox.shape), _const_spec(w_hgrn.shape), _const_spec(bf_pad.shape),
        _const_spec(qg2.shape), _const_spec(kg2.shape), _const_spec(lb_logits.shape), _const_spec(gnorm.shape),
        pl.BlockSpec((1, 1, LANES), proj_seq),
        pl.BlockSpec((1, nha, dk, dv), rec_seq),
    ]
    nsl = w_hgrn.shape[0]
    return pl.pallas_call(
        functools.partial(_mixer_in_kernel, layer=layer, aw=aw, bw=bw, nh=nh, dh=dh, nha=nha, rb=rb,
                          tiles_per_seq=tps, n_tiles=n_tiles),
        out_shape=out_shape,
        grid=(n_tiles + 1,),
        in_specs=in_specs,
        out_specs=out_specs,
        scratch_shapes=[pltpu.VMEM((ta, d), BF16), pltpu.VMEM((ta, w_fox.shape[1]), F32),
                        pltpu.VMEM((nsl, ta, MXU_COLS), F32), pltpu.VMEM((nsl, ta, MXU_COLS), F32),
                        pltpu.VMEM((ta, LANES), F32), pltpu.VMEM((ta, LANES), F32),
                        pltpu.VMEM((1, LANES), F32), pltpu.VMEM((nha, dv, dk), F32)],
        compiler_params=pltpu.CompilerParams(dimension_semantics=("arbitrary",), vmem_limit_bytes=VMEM_LIMIT),
        name="mixer_in",
    )(x, n1, w_fox, w_hgrn, bf_pad, qg2, kg2, lb_logits, gnorm, carry_in, s0)


def _one_head(q_pair, odd, dh):
    lane = lax.broadcasted_iota(jnp.int32, q_pair.shape, 1)
    keep = lane >= dh if odd else lane < dh
    return jnp.where(keep, q_pair, jnp.zeros_like(q_pair))


def _head_column(c_block, lane_idx):
    lane = lax.broadcasted_iota(jnp.int32, c_block.shape, 1)
    col = jnp.sum(jnp.where(lane == lane_idx, c_block, 0.0), axis=-1, keepdims=True)
    return jnp.broadcast_to(col, c_block.shape)


def _flash_step(q, k, v, ck, cq, m_ref, l_ref, acc_ref, hh, rows, mask):
    t = lax.dot_general(q, k, _NT, preferred_element_type=F32) - ck
    if mask is not None:
        t = jnp.where(mask, t, NEG_BIG)
    m_prev = m_ref[hh, rows, :]
    m_new = jnp.maximum(m_prev, jnp.max(t, axis=-1, keepdims=True) + cq)
    alpha = jnp.exp2(m_prev - m_new)
    off = m_new - cq
    ps = [jnp.exp2(t[:, LANES * c:LANES * (c + 1)] - off) for c in range(t.shape[1] // LANES)]
    l_ref[hh, rows, :] = alpha * l_ref[hh, rows, :] + sum(ps[1:], ps[0])
    p = jnp.concatenate(ps, axis=1).astype(BF16)
    acc_ref[hh, rows, :] = alpha * acc_ref[hh, rows, :] + jnp.dot(p, v, preferred_element_type=F32)
    m_ref[hh, rows, :] = m_new


def _fixed_offset_step(q, k, v, ck, cq, l_ref, acc_ref, hh, rows, mask):
    s = lax.dot_general(q, k, _NT, preferred_element_type=F32)
    ps = []
    for c in range(s.shape[1] // LANES):
        cols = slice(LANES * c, LANES * (c + 1))
        t = s[:, cols] - ck[:, cols] + cq
        if mask is not None:
            t = jnp.where(mask[:, cols], t, NEG_BIG)
        ps.append(jnp.exp2(t))
    l_ref[hh, rows, :] += sum(ps[1:], ps[0])
    p = jnp.concatenate(ps, axis=1).astype(BF16)
    acc_ref[hh, rows, :] += jnp.dot(p, v, preferred_element_type=F32)


def _merge_pair(acc0, l0, acc1, l1, dh):
    o0 = acc0 / jnp.sum(l0, axis=-1, keepdims=True)
    o1 = acc1 / jnp.sum(l1, axis=-1, keepdims=True)
    lane = lax.broadcasted_iota(jnp.int32, o0.shape, 1)
    return jnp.where(lane < dh, o0, o1)


def _attn_kernel(cb_ref, thr_ref, q_ref, k_ref, v_ref, crow_ref, ccol_ref, o_ref, m_scr, l_scr, acc_scr,
                 *, tq, dh, nh, hps, online):
    bi, g, i = pl.program_id(0), pl.program_id(1), pl.program_id(2)
    nblk = pl.num_programs(2)
    acc_scr[...] = jnp.zeros(acc_scr.shape, F32)
    l_scr[...] = jnp.zeros(l_scr.shape, F32)
    if online:
        m_scr[...] = jnp.full(m_scr.shape, NEG_BIG, F32)

    def step(q, k, v, ck, cq, hh, rows, mask):
        if online:
            _flash_step(q, k, v, ck, cq, m_scr, l_scr, acc_scr, hh, rows, mask)
        else:
            _fixed_offset_step(q, k, v, ck, cq, l_scr, acc_scr, hh, rows, mask)

    def first_live_block(hd):
        base = (bi * nh + hd) * nblk
        c_before = cb_ref[base + jnp.maximum(i - 1, 0)]

        def previous_is_live(n):
            c_end = cb_ref[base + jnp.maximum(n - 1, 0)]
            return jnp.logical_and(n > 0, (c_before - c_end) * LOG2E + thr_ref[0] >= EXP2_ZERO_BELOW)

        return lax.while_loop(previous_is_live, lambda n: n - 1, i)

    j_lo = first_live_block(hps * g)
    for hh in range(1, hps):
        j_lo = jnp.minimum(j_lo, first_live_block(hps * g + hh))

    def q_of(hh, rows=slice(None)):
        return _one_head(q_ref[0, rows, LANES * (hh // 2):LANES * (hh // 2 + 1)], hh % 2, dh)

    def cq_of(hh, rows=slice(None)):
        return _head_column(ccol_ref[0, rows, :], hps * g + hh)

    def body(j, _):
        keys = pl.ds(pl.multiple_of(j * tq, tq), tq)
        for hh in range(hps):
            kcols = slice(LANES * (hh // 2), LANES * (hh // 2 + 1))
            step(q_of(hh), k_ref[0, keys, kcols], v_ref[0, keys, kcols], crow_ref[0, j, 0, hh:hh + 1, :],
                 cq_of(hh), hh, slice(0, tq), None)
        return 0

    lax.fori_loop(j_lo, i, body, 0)

    half = tq // 2
    keys_a = pl.ds(pl.multiple_of(i * tq, tq), half)
    keys_b = pl.ds(pl.multiple_of(i * tq + half, half), half)
    mask_a = (lax.broadcasted_iota(jnp.int32, (tq, half), 0) >= lax.broadcasted_iota(jnp.int32, (tq, half), 1))
    mask_b = (lax.broadcasted_iota(jnp.int32, (half, half), 0) >= lax.broadcasted_iota(jnp.int32, (half, half), 1))
    late = slice(half, tq)
    for hh in range(hps):
        kcols = slice(LANES * (hh // 2), LANES * (hh // 2 + 1))
        step(q_of(hh), k_ref[0, keys_a, kcols], v_ref[0, keys_a, kcols], crow_ref[0, i, 0, hh:hh + 1, :half],
             cq_of(hh), hh, slice(0, tq), mask_a)
        step(q_of(hh, late), k_ref[0, keys_b, kcols], v_ref[0, keys_b, kcols],
             crow_ref[0, i, 0, hh:hh + 1, half:], cq_of(hh, late), hh, late, mask_b)
    for p in range(hps // 2):
        o_ref[0, :, LANES * p:LANES * (p + 1)] = _merge_pair(
            acc_scr[2 * p], l_scr[2 * p], acc_scr[2 * p + 1], l_scr[2 * p + 1], dh).astype(o_ref.dtype)


def _attn(cb, thr, qb, kb, vb, crow, ccol, *, nh, dh, online):
    b, t, _ = qb.shape
    tq = min(TOKEN_TILE, t)
    hps = min(ATTN_HEADS_PER_STEP, nh)
    ng = nh // hps
    crow5 = crow.reshape(b, t // tq, ng, hps, tq)
    grid_spec = pltpu.PrefetchScalarGridSpec(
        num_scalar_prefetch=2,
        grid=(b, ng, t // tq),
        in_specs=[pl.BlockSpec((1, tq, hps * dh), lambda bi, g, i, *_: (bi, i, g)),
                  pl.BlockSpec((1, t, hps * dh), lambda bi, g, i, *_: (bi, 0, g)),
                  pl.BlockSpec((1, t, hps * dh), lambda bi, g, i, *_: (bi, 0, g)),
                  pl.BlockSpec((1, t // tq, 1, hps, tq), lambda bi, g, i, *_: (bi, 0, g, 0, 0)),
                  pl.BlockSpec((1, tq, LANES), lambda bi, g, i, *_: (bi, i, 0))],
        out_specs=pl.BlockSpec((1, tq, hps * dh), lambda bi, g, i, *_: (bi, i, g)),
        scratch_shapes=[pltpu.VMEM((hps, tq, LANES), F32), pltpu.VMEM((hps, tq, LANES), F32),
                        pltpu.VMEM((hps, tq, LANES), F32)],
    )
    return pl.pallas_call(
        functools.partial(_attn_kernel, tq=tq, dh=dh, nh=nh, hps=hps, online=online),
        out_shape=jax.ShapeDtypeStruct((b, t, nh * dh), BF16),
        grid_spec=grid_spec,
        compiler_params=pltpu.CompilerParams(
            dimension_semantics=("arbitrary", "arbitrary", "arbitrary"), vmem_limit_bytes=VMEM_LIMIT),
        name="fox_attn_online" if online else "fox_attn",
    )(cb, thr, qb, kb, vb, crow5, ccol)


def _sample_attn_kernel(q_ref, kn_ref, vn_ref, crn_ref, ccol_ref, kc_ref, vc_ref, clf_ref, o_ref, c_scr,
                        *, nh, dh, rb):
    s_len, p_len = q_ref.shape[1], kc_ref.shape[3]

    def cache_rows(r, carry):
        rows = pl.ds(pl.multiple_of(r * rb, rb), rb)
        c = _cumsum_rows(clf_ref[0, rows, :]) + carry
        c_scr[rows, :] = c
        return c[rb - 1:rb, :]

    c_end = lax.fori_loop(0, p_len // rb, cache_rows, jnp.zeros((1, LANES), F32))
    c_cache = ((c_scr[...] - c_end) * LOG2E).T[:nh, :]

    causal = (lax.broadcasted_iota(jnp.int32, (s_len, s_len), 0)
              >= lax.broadcasted_iota(jnp.int32, (s_len, s_len), 1))
    for hd in range(nh):
        cols = slice(dh * hd, dh * (hd + 1))
        q = q_ref[0, :, cols]
        cq = ccol_ref[0, :, hd:hd + 1]
        kc_t = kc_ref[0, hd].astype(BF16)
        vc_t = vc_ref[0, hd].astype(BF16)
        t1 = jnp.dot(q, kc_t, preferred_element_type=F32) - c_cache[hd:hd + 1, :] + cq
        t2 = lax.dot_general(q, kn_ref[0, :, cols], _NT, preferred_element_type=F32) - crn_ref[0, 0, hd:hd + 1, :] + cq
        t2 = jnp.where(causal, t2, NEG_BIG)
        m = jnp.maximum(jnp.max(t1, axis=-1, keepdims=True), jnp.max(t2, axis=-1, keepdims=True))
        p1 = jnp.exp2(t1 - m)
        p2 = jnp.exp2(t2 - m)
        den = jnp.sum(p1, axis=-1, keepdims=True) + jnp.sum(p2, axis=-1, keepdims=True)
        acc = (lax.dot_general(p1.astype(BF16), vc_t, _NT, preferred_element_type=F32)
               + jnp.dot(p2.astype(BF16), vn_ref[0, :, cols], preferred_element_type=F32))
        o_ref[0, :, cols] = (acc / den).astype(o_ref.dtype)


def _sample_attn(qb, kb_n, vb_n, crow_n, ccol, cache_kt, cache_vt, clf_pad, *, nh, dh):
    b, s_len, bw = qb.shape
    p_len = cache_kt.shape[3]
    whole = lambda bi: (bi, 0, 0)
    whole4 = lambda bi: (bi, 0, 0, 0)
    return pl.pallas_call(
        functools.partial(_sample_attn_kernel, nh=nh, dh=dh, rb=min(HGRN_CHUNK, p_len)),
        out_shape=jax.ShapeDtypeStruct((b, s_len, bw), BF16),
        grid=(b,),
        in_specs=[pl.BlockSpec((1, s_len, bw), whole), pl.BlockSpec((1, s_len, bw), whole),
                  pl.BlockSpec((1, s_len, bw), whole), pl.BlockSpec((1, 1, nh, s_len), whole4),
                  pl.BlockSpec((1, s_len, LANES), whole),
                  pl.BlockSpec((1, nh, dh, p_len), whole4), pl.BlockSpec((1, nh, dh, p_len), whole4),
                  pl.BlockSpec((1, p_len, LANES), whole)],
        out_specs=pl.BlockSpec((1, s_len, bw), whole),
        scratch_shapes=[pltpu.VMEM((p_len, LANES), F32)],
        compiler_params=pltpu.CompilerParams(dimension_semantics=("arbitrary",), vmem_limit_bytes=VMEM_LIMIT),
        name="fox_sample_attn",
    )(qb, kb_n, vb_n, crow_n, ccol, cache_kt, cache_vt, clf_pad)


def _ffn_kernel(x_ref, oa_ref, ob_ref, woa_ref, wob_ref, n2_ref, wup_ref, wdn_ref, y_ref, h_scr, *, ff_chunk):
    x1 = (x_ref[...]
          + jnp.dot(oa_ref[...], woa_ref[...], preferred_element_type=F32)
          + jnp.dot(ob_ref[...], wob_ref[...], preferred_element_type=F32))
    ms = jnp.mean(x1 * x1, axis=-1, keepdims=True)
    h_scr[...] = (x1 * lax.rsqrt(ms + EPS) * n2_ref[...]).astype(BF16)
    y_ref[...] = x1
    for c in range(wup_ref.shape[1] // ff_chunk):
        cs = slice(ff_chunk * c, ff_chunk * (c + 1))
        u = jnp.maximum(jnp.dot(h_scr[...], wup_ref[:, cs], preferred_element_type=F32), 0.0)
        y_ref[...] += jnp.dot((u * u).astype(BF16), wdn_ref[cs, :], preferred_element_type=F32)


def _ffn(x2d, oa2d, ob2d, woa, wob, n2, wup, wdn):
    n, d = x2d.shape
    tc = min(TOKEN_TILE, n)
    aw, bw = oa2d.shape[1], ob2d.shape[1]
    row = lambda i: (i, 0)
    return pl.pallas_call(
        functools.partial(_ffn_kernel, ff_chunk=min(1024, wup.shape[1])),
        out_shape=jax.ShapeDtypeStruct((n, d), F32),
        grid=(n // tc,),
        in_specs=[pl.BlockSpec((tc, d), row), pl.BlockSpec((tc, aw), row), pl.BlockSpec((tc, bw), row),
                  _const_spec(woa.shape), _const_spec(wob.shape), _const_spec(n2.shape),
                  _const_spec(wup.shape), _const_spec(wdn.shape)],
        out_specs=pl.BlockSpec((tc, d), row),
        scratch_shapes=[pltpu.VMEM((tc, d), BF16)],
        compiler_params=pltpu.CompilerParams(
            dimension_semantics=("arbitrary",), vmem_limit_bytes=VMEM_LIMIT),
        name="out_ffn",
    )(x2d, oa2d, ob2d, woa, wob, n2, wup, wdn)


def kernel(x_prompt, x_sample, cache_fox_k, cache_fox_v, cache_fox_logf, state_hgrn, norm1, w_in, b_fox_f,
           q_norm_gain, k_norm_gain, hgrn_lb_logits, hgrn_out_norm, w_out, norm2, w_up, w_down):
    depth = w_in.shape[0]
    d = x_prompt.shape[-1]
    nh, dh = cache_fox_k.shape[3], cache_fox_k.shape[4]
    nha, dk, dv = state_hgrn.shape[2], state_hgrn.shape[3], state_hgrn.shape[4]
    aw, bw = nha * dk, nh * dh
    assert dk == LANES and dv == LANES and 2 * dh == LANES and nh % 2 == 0 and nh <= 8
    assert w_in.shape[2] == 4 * aw + 3 * bw + nh and aw + bw == w_out.shape[1] and aw % MXU_COLS == 0

    xp, xs = x_prompt, x_sample
    bp, tp, _ = xp.shape
    bs, ts, _ = xs.shape
    plen = cache_fox_k.shape[2]
    lb_logits = hgrn_lb_logits.astype(F32)

    kp, vp, lp, sp = [], [], [], []
    ksl, vsl, lsl, ssl = [], [], [], []
    for l in range(depth):
        w_fox = jnp.pad(w_in[l, :, 4 * aw:], ((0, 0), (0, LANES - nh))).astype(BF16)
        w_hgrn = jnp.transpose(w_in[l, :, :4 * aw].reshape(d, 4 * aw // MXU_COLS, MXU_COLS), (1, 0, 2)).astype(BF16)
        bf_pad = jnp.pad(b_fox_f[l].astype(F32), (0, LANES - nh)).reshape(1, LANES)
        qg2 = jnp.tile(q_norm_gain[l].astype(F32), 2).reshape(1, LANES)
        kg2 = jnp.tile(k_norm_gain[l].astype(F32), 2).reshape(1, LANES)
        n1 = norm1[l].astype(F32).reshape(1, d)
        n2 = norm2[l].astype(F32).reshape(1, d)
        gnorm = hgrn_out_norm[l].astype(F32).reshape(1, aw)
        woa = w_out[l, :aw].astype(BF16)
        wob = w_out[l, aw:].astype(BF16)
        wup = w_up[l].astype(BF16)
        wdn = w_down[l].astype(BF16)
        mixer_in = functools.partial(_mixer_in, layer=l, nh=nh, dh=dh)

        (k_p, v_p, lf_p, qb, kb, vb, ccol, crow, _, cends, oa, s_p) = mixer_in(
            xp, n1, w_fox, w_hgrn, bf_pad, qg2, kg2, lb_logits, gnorm, jnp.zeros((bp, 1, LANES), F32),
            jnp.zeros((bp, nha, dk, dv), F32), chunk=min(HGRN_CHUNK, tp))
        per_blk = cends.shape[1] // (tp // min(TOKEN_TILE, tp))
        cb = jnp.transpose(cends[:, per_blk - 1::per_blk, :nh], (0, 2, 1)).reshape(-1)
        qk_bound = dh ** 0.5 * jnp.max(jnp.abs(q_norm_gain[l])) * jnp.max(jnp.abs(k_norm_gain[l]))
        thr = (2.04 * LOG2E * qk_bound + 2.0).astype(F32).reshape(1)
        ob = lax.cond(thr[0] <= 2.0 * FIXED_OFFSET_MAX_LOGIT,
                      functools.partial(_attn, nh=nh, dh=dh, online=False),
                      functools.partial(_attn, nh=nh, dh=dh, online=True),
                      cb, thr, qb, kb, vb, crow, ccol)
        xp = _ffn(xp.reshape(bp * tp, d), oa.reshape(bp * tp, aw), ob.reshape(bp * tp, bw),
                  woa, wob, n2, wup, wdn).reshape(bp, tp, d)
        kp.append(k_p.reshape(bp, tp, nh, dh))
        vp.append(v_p.reshape(bp, tp, nh, dh))
        lp.append(jnp.transpose(lf_p, (0, 2, 1)))
        sp.append(s_p.astype(x_prompt.dtype))

        clf_pad = jnp.pad(cache_fox_logf[l].astype(F32), ((0, 0), (0, 0), (0, LANES - nh)))
        (k_s, v_s, lf_s, qb, kb, vb, ccol, crow, _, _, oa, s_s) = mixer_in(
            xs, n1, w_fox, w_hgrn, bf_pad, qg2, kg2, lb_logits, gnorm, jnp.zeros((bs, 1, LANES), F32),
            state_hgrn[l].astype(F32), chunk=ts)
        ob = _sample_attn(qb, kb, vb, crow, ccol, jnp.transpose(cache_fox_k[l].astype(F32), (0, 2, 3, 1)),
                          jnp.transpose(cache_fox_v[l].astype(F32), (0, 2, 3, 1)), clf_pad, nh=nh, dh=dh)
        xs = _ffn(xs.reshape(bs * ts, d), oa.reshape(bs * ts, aw), ob.reshape(bs * ts, bw),
                  woa, wob, n2, wup, wdn).reshape(bs, ts, d)
        ksl.append(k_s.reshape(bs, ts, nh, dh))
        vsl.append(v_s.reshape(bs, ts, nh, dh))
        lsl.append(jnp.transpose(lf_s, (0, 2, 1)))
        ssl.append(s_s.astype(state_hgrn.dtype))

    return (xp, xs, jnp.stack(kp), jnp.stack(vp), jnp.stack(lp), jnp.stack(sp),
            jnp.stack(ksl), jnp.stack(vsl), jnp.stack(lsl), jnp.stack(ssl))
```

```python
import functools

import jax
import jax.numpy as jnp
from jax import lax
from jax.experimental import pallas as pl
from jax.experimental.pallas import tpu as pltpu

F32 = jnp.float32
BF16 = jnp.bfloat16

EPS = 1e-6
HGRN_CHUNK = 64
TOKEN_TILE = 512
ATTN_HEADS_PER_STEP = 8
LANES = 128
MXU_COLS = 256
NEG_BIG = -1e30
LOG2E = 1.4426950408889634
EXP2_ZERO_BELOW = -150.0
FIXED_OFFSET_MAX_LOGIT = 60.0
VMEM_LIMIT = 56 * 1024 * 1024

_NT = (((1,), (1,)), ((), ()))
_TN = (((0,), (0,)), ((), ()))


def _const_spec(shape):
    nd = len(shape)
    return pl.BlockSpec(shape, lambda *_: (0,) * nd, pipeline_mode=pl.Buffered(1))


def _cumsum_rows(x):
    n = x.shape[0]
    row = lax.broadcasted_iota(jnp.int32, x.shape, 0)
    s = 1
    while s < n:
        x = x + jnp.where(row >= s, pltpu.roll(x, s, axis=0), 0.0)
        s *= 2
    return x


def _pair_rms_norm(z, gain2, dh):
    lane = lax.broadcasted_iota(jnp.int32, z.shape, 1)
    first = lane < dh
    sq = z * z
    s0 = jnp.sum(jnp.where(first, sq, 0.0), axis=-1, keepdims=True)
    s1 = jnp.sum(jnp.where(first, 0.0, sq), axis=-1, keepdims=True)
    ms = jnp.where(first, s0, s1) * (1.0 / dh)
    return z * lax.rsqrt(ms + EPS) * gain2


def _log_sigmoid(x):
    return jnp.minimum(x, 0.0) - jnp.log(1.0 + jnp.exp(-jnp.abs(x)))


def _hgrn_chunk(q, fa, ia, ga, lb, gn, st_ref, hd, causal):
    chunk = q.shape[0]
    f = lb + (1.0 - lb) * jax.nn.sigmoid(fa)
    k = 1.0 - f
    b = _cumsum_rows(jnp.log(f))
    b_ref = b[chunk // 2:chunk // 2 + 1, :]
    b_last = b[chunk - 1:chunk, :]
    v = ia.astype(BF16)
    st = st_ref[hd]
    inter = lax.dot_general((q * jnp.exp(b)).astype(BF16), st.astype(BF16), _NT, preferred_element_type=F32)
    qr = (q * jnp.exp(b - b_ref)).astype(BF16)
    kr = (k * jnp.exp(b_ref - b)).astype(BF16)
    a = lax.dot_general(qr, kr, _NT, preferred_element_type=F32)
    a = jnp.where(causal, a, 0.0).astype(BF16)
    o = inter + jnp.dot(a, v, preferred_element_type=F32)
    k3 = (k * jnp.exp(b_last - b)).astype(BF16)
    st_ref[hd] = st * jnp.exp(b_last) + lax.dot_general(v, k3, _TN, preferred_element_type=F32)
    ms = jnp.mean(o * o, axis=-1, keepdims=True)
    return (o * lax.rsqrt(ms + EPS) * gn * (ga * jax.nn.sigmoid(ga))).astype(BF16)


def _mixer_in_kernel(x_ref, n1_ref, w_ref, bf_ref, qg_ref, kg_ref, lbl_ref, gn_ref, s0_ref,
                     k_out, v_out, lf_out, qb_out, kb_out, vb_out, ccol_out, crow_out, cends_out,
                     oa_out, s_out,
                     h_scr, z_even, z_odd, c_scr, lf_scr, carry_scr, st_scr,
                     *, layer, aw, bw, nh, dh, nha, rb, tiles_per_seq):
    t = pl.program_id(0)
    ih = jnp.maximum(t - 1, 0) % tiles_per_seq

    @pl.when(t == 0)
    def _():
        z_odd[...] = jnp.zeros(z_odd.shape, F32)
        st_scr[...] = jnp.zeros(st_scr.shape, F32)
        carry_scr[...] = jnp.zeros(carry_scr.shape, F32)

    @pl.when(jnp.logical_and(ih == 0, t >= 1))
    def _():
        carry_scr[...] = jnp.zeros(carry_scr.shape, F32)
        for hd in range(nha):
            st_scr[hd] = s0_ref[0, hd].T

    x = x_ref[0]
    ta = x.shape[0]
    nrb = ta // rb
    nsl = w_ref.shape[0]
    ms = jnp.mean(x * x, axis=-1, keepdims=True)
    h_scr[...] = (x * lax.rsqrt(ms + EPS) * n1_ref[...]).astype(BF16)

    lg = lbl_ref[...]
    e = jnp.exp(lg - jnp.max(lg, axis=0, keepdims=True))
    lb = jnp.sum(e[:layer + 1], axis=0, keepdims=True) / jnp.sum(e, axis=0, keepdims=True)

    scale = dh ** -0.5 * LOG2E
    causal = (lax.broadcasted_iota(jnp.int32, (rb, rb), 0) >= lax.broadcasted_iota(jnp.int32, (rb, rb), 1))
    hps = MXU_COLS // LANES
    spw = aw // MXU_COLS
    fox = 4 * spw
    spb = bw // MXU_COLS

    def trip(z_new, z_old, r, carry):
        rows = slice(r * rb, (r + 1) * rb)

        lf = _log_sigmoid(z_old[fox + 3 * spb, rows, :LANES] + bf_ref[...])
        lf_scr[rows, :] = lf
        c = _cumsum_rows(lf) + carry
        c2 = c * LOG2E
        c_scr[rows, :] = c2
        ccol_out[0, rows, :] = c2
        for p in range(nh // 2):
            sl, ln = p // hps, slice(LANES * (p % hps), LANES * (p % hps + 1))
            cols = slice(LANES * p, LANES * (p + 1))
            qn = _pair_rms_norm(z_old[fox + sl, rows, ln], qg_ref[...], dh) * scale
            kn = _pair_rms_norm(z_old[fox + spb + sl, rows, ln], kg_ref[...], dh)
            vv = z_old[fox + 2 * spb + sl, rows, ln]
            qb_out[0, rows, cols] = qn.astype(BF16)
            kb_out[0, rows, cols] = kn.astype(BF16)
            vb_out[0, rows, cols] = vv.astype(BF16)
            k_out[0, rows, cols] = kn
            v_out[0, rows, cols] = vv

        for hd in range(nha):
            sl, ln = hd // hps, slice(LANES * (hd % hps), LANES * (hd % hps + 1))
            cols = slice(LANES * hd, LANES * (hd + 1))
            oa_out[0, rows, cols] = _hgrn_chunk(
                z_old[sl, rows, ln], z_old[spw + sl, rows, ln],
                z_old[2 * spw + sl, rows, ln], z_old[3 * spw + sl, rows, ln],
                lb[:, cols], gn_ref[:, cols], st_scr, hd, causal)

        for sl in range(r * nsl // nrb, (r + 1) * nsl // nrb):
            z_new[sl] = jnp.dot(h_scr[...], w_ref[sl], preferred_element_type=F32)
        return c[rb - 1:rb, :]

    def run(z_new, z_old):
        carry = carry_scr[...]
        ends = []
        for r in range(nrb):
            carry = trip(z_new, z_old, r, carry)
            ends.append(carry)
        carry_scr[...] = carry
        cends_out[0] = jnp.concatenate(ends, axis=0)

    pl.when(t % 2 == 0)(lambda: run(z_even, z_odd))
    pl.when(t % 2 == 1)(lambda: run(z_odd, z_even))
    crow_out[0, 0] = c_scr[...].T[:nh, :]
    lf_out[0] = lf_scr[...].T[:nh, :]

    @pl.when(ih == tiles_per_seq - 1)
    def _():
        for hd in range(nha):
            s_out[0, hd] = st_scr[hd].T


def _mixer_in(x, n1, w_all, bf_pad, qg2, kg2, lb_logits, gnorm, s0, *, layer, nh, dh, chunk):
    b, t, d = x.shape
    nha, dk, dv = s0.shape[1], s0.shape[2], s0.shape[3]
    aw, bw = nha * dk, nh * dh
    ta = min(TOKEN_TILE, t)
    rb = chunk
    tps = t // ta
    n_tiles = b * tps
    nsl = w_all.shape[0]
    assert ta % rb == 0 and t % ta == 0 and nsl == (4 * aw + 3 * bw) // MXU_COLS + 1

    def done_side(*tail):
        return lambda s: (jnp.maximum(s - 1, 0) // tps, jnp.maximum(s - 1, 0) % tps) + tail

    done_seq = lambda s: (jnp.maximum(s - 1, 0) // tps, 0, 0, 0)
    tok = done_side(0)
    out_shape = (
        jax.ShapeDtypeStruct((b, t, bw), F32),
        jax.ShapeDtypeStruct((b, t, bw), F32),
        jax.ShapeDtypeStruct((b, nh, t), F32),
        jax.ShapeDtypeStruct((b, t, bw), BF16),
        jax.ShapeDtypeStruct((b, t, bw), BF16),
        jax.ShapeDtypeStruct((b, t, bw), BF16),
        jax.ShapeDtypeStruct((b, t, LANES), F32),
        jax.ShapeDtypeStruct((b, tps, nh, ta), F32),
        jax.ShapeDtypeStruct((b, t // rb, LANES), F32),
        jax.ShapeDtypeStruct((b, t, aw), BF16),
        jax.ShapeDtypeStruct((b, nha, dk, dv), F32),
    )
    out_specs = (
        pl.BlockSpec((1, ta, bw), tok), pl.BlockSpec((1, ta, bw), tok),
        pl.BlockSpec((1, nh, ta), lambda s: (jnp.maximum(s - 1, 0) // tps, 0, jnp.maximum(s - 1, 0) % tps)),
        pl.BlockSpec((1, ta, bw), tok), pl.BlockSpec((1, ta, bw), tok), pl.BlockSpec((1, ta, bw), tok),
        pl.BlockSpec((1, ta, LANES), tok),
        pl.BlockSpec((1, 1, nh, ta), done_side(0, 0)),
        pl.BlockSpec((1, ta // rb, LANES), tok),
        pl.BlockSpec((1, ta, aw), tok),
        pl.BlockSpec((1, nha, dk, dv), done_seq),
    )
    in_specs = [
        pl.BlockSpec((1, ta, d), lambda s: (jnp.minimum(s, n_tiles - 1) // tps, jnp.minimum(s, n_tiles - 1) % tps, 0)),
        _const_spec(n1.shape), _const_spec(w_all.shape), _const_spec(bf_pad.shape),
        _const_spec(qg2.shape), _const_spec(kg2.shape), _const_spec(lb_logits.shape), _const_spec(gnorm.shape),
        pl.BlockSpec((1, nha, dk, dv), done_seq),
    ]
    return pl.pallas_call(
        functools.partial(_mixer_in_kernel, layer=layer, aw=aw, bw=bw, nh=nh, dh=dh, nha=nha, rb=rb,
                          tiles_per_seq=tps),
        out_shape=out_shape,
        grid=(n_tiles + 1,),
        in_specs=in_specs,
        out_specs=out_specs,
        scratch_shapes=[pltpu.VMEM((ta, d), BF16),
                        pltpu.VMEM((nsl, ta, MXU_COLS), F32), pltpu.VMEM((nsl, ta, MXU_COLS), F32),
                        pltpu.VMEM((ta, LANES), F32), pltpu.VMEM((ta, LANES), F32),
                        pltpu.VMEM((1, LANES), F32), pltpu.VMEM((nha, dv, dk), F32)],
        compiler_params=pltpu.CompilerParams(dimension_semantics=("arbitrary",), vmem_limit_bytes=VMEM_LIMIT),
        name="mixer_in",
    )(x, n1, w_all, bf_pad, qg2, kg2, lb_logits, gnorm, s0)


def _one_head(q_pair, odd, dh):
    lane = lax.broadcasted_iota(jnp.int32, q_pair.shape, 1)
    keep = lane >= dh if odd else lane < dh
    return jnp.where(keep, q_pair, jnp.zeros_like(q_pair))


def _head_column(c_block, lane_idx):
    lane = lax.broadcasted_iota(jnp.int32, c_block.shape, 1)
    col = jnp.sum(jnp.where(lane == lane_idx, c_block, 0.0), axis=-1, keepdims=True)
    return jnp.broadcast_to(col, c_block.shape)


def _flash_step(q, k, v, ck, cq, m_ref, l_ref, acc_ref, hh, rows, mask):
    t = lax.dot_general(q, k, _NT, preferred_element_type=F32) - ck
    if mask is not None:
        t = jnp.where(mask, t, NEG_BIG)
    m_prev = m_ref[hh, rows, :]
    m_new = jnp.maximum(m_prev, jnp.max(t, axis=-1, keepdims=True) + cq)
    alpha = jnp.exp2(m_prev - m_new)
    off = m_new - cq
    ps = [jnp.exp2(t[:, LANES * c:LANES * (c + 1)] - off) for c in range(t.shape[1] // LANES)]
    l_ref[hh, rows, :] = alpha * l_ref[hh, rows, :] + sum(ps[1:], ps[0])
    p = jnp.concatenate(ps, axis=1).astype(BF16)
    acc_ref[hh, rows, :] = alpha * acc_ref[hh, rows, :] + jnp.dot(p, v, preferred_element_type=F32)
    m_ref[hh, rows, :] = m_new


def _fixed_offset_step(q, k, v, ck, cq, l_ref, acc_ref, hh, rows, mask):
    s = lax.dot_general(q, k, _NT, preferred_element_type=F32)
    ps = []
    for c in range(s.shape[1] // LANES):
        cols = slice(LANES * c, LANES * (c + 1))
        t = s[:, cols] - ck[:, cols] + cq
        if mask is not None:
            t = jnp.where(mask[:, cols], t, NEG_BIG)
        ps.append(jnp.exp2(t))
    l_ref[hh, rows, :] += sum(ps[1:], ps[0])
    p = jnp.concatenate(ps, axis=1).astype(BF16)
    acc_ref[hh, rows, :] += jnp.dot(p, v, preferred_element_type=F32)


def _merge_pair(acc0, l0, acc1, l1, dh):
    o0 = acc0 / jnp.sum(l0, axis=-1, keepdims=True)
    o1 = acc1 / jnp.sum(l1, axis=-1, keepdims=True)
    lane = lax.broadcasted_iota(jnp.int32, o0.shape, 1)
    return jnp.where(lane < dh, o0, o1)


def _attn_kernel(cb_ref, thr_ref, q_ref, k_ref, v_ref, crow_ref, ccol_ref, o_ref, m_scr, l_scr, acc_scr,
                 *, tq, dh, nh, hps, online):
    bi, g, i = pl.program_id(0), pl.program_id(1), pl.program_id(2)
    nblk = pl.num_programs(2)
    acc_scr[...] = jnp.zeros(acc_scr.shape, F32)
    l_scr[...] = jnp.zeros(l_scr.shape, F32)
    if online:
        m_scr[...] = jnp.full(m_scr.shape, NEG_BIG, F32)

    def step(q, k, v, ck, cq, hh, rows, mask):
        if online:
            _flash_step(q, k, v, ck, cq, m_scr, l_scr, acc_scr, hh, rows, mask)
        else:
            _fixed_offset_step(q, k, v, ck, cq, l_scr, acc_scr, hh, rows, mask)

    def first_live_block(hd):
        base = (bi * nh + hd) * nblk
        c_before = cb_ref[base + jnp.maximum(i - 1, 0)]

        def previous_is_live(n):
            c_end = cb_ref[base + jnp.maximum(n - 1, 0)]
            return jnp.logical_and(n > 0, (c_before - c_end) * LOG2E + thr_ref[0] >= EXP2_ZERO_BELOW)

        return lax.while_loop(previous_is_live, lambda n: n - 1, i)

    j_lo = first_live_block(hps * g)
    for hh in range(1, hps):
        j_lo = jnp.minimum(j_lo, first_live_block(hps * g + hh))

    def q_of(hh, rows=slice(None)):
        return _one_head(q_ref[0, rows, LANES * (hh // 2):LANES * (hh // 2 + 1)], hh % 2, dh)

    def cq_of(hh, rows=slice(None)):
        return _head_column(ccol_ref[0, rows, :], hps * g + hh)

    def body(j, _):
        keys = pl.ds(pl.multiple_of(j * tq, tq), tq)
        for hh in range(hps):
            kcols = slice(LANES * (hh // 2), LANES * (hh // 2 + 1))
            step(q_of(hh), k_ref[0, keys, kcols], v_ref[0, keys, kcols], crow_ref[0, j, 0, hh:hh + 1, :],
                 cq_of(hh), hh, slice(0, tq), None)
        return 0

    lax.fori_loop(j_lo, i, body, 0)

    half = tq // 2
    keys_a = pl.ds(pl.multiple_of(i * tq, tq), half)
    keys_b = pl.ds(pl.multiple_of(i * tq + half, half), half)
    mask_a = (lax.broadcasted_iota(jnp.int32, (tq, half), 0) >= lax.broadcasted_iota(jnp.int32, (tq, half), 1))
    mask_b = (lax.broadcasted_iota(jnp.int32, (half, half), 0) >= lax.broadcasted_iota(jnp.int32, (half, half), 1))
    late = slice(half, tq)
    for hh in range(hps):
        kcols = slice(LANES * (hh // 2), LANES * (hh // 2 + 1))
        step(q_of(hh), k_ref[0, keys_a, kcols], v_ref[0, keys_a, kcols], crow_ref[0, i, 0, hh:hh + 1, :half],
             cq_of(hh), hh, slice(0, tq), mask_a)
        step(q_of(hh, late), k_ref[0, keys_b, kcols], v_ref[0, keys_b, kcols],
             crow_ref[0, i, 0, hh:hh + 1, half:], cq_of(hh, late), hh, late, mask_b)
    for p in range(hps // 2):
        o_ref[0, :, LANES * p:LANES * (p + 1)] = _merge_pair(
            acc_scr[2 * p], l_scr[2 * p], acc_scr[2 * p + 1], l_scr[2 * p + 1], dh).astype(o_ref.dtype)


def _attn(cb, thr, qb, kb, vb, crow, ccol, *, nh, dh, online):
    b, t, _ = qb.shape
    tq = min(TOKEN_TILE, t)
    hps = min(ATTN_HEADS_PER_STEP, nh)
    ng = nh // hps
    crow5 = crow.reshape(b, t // tq, ng, hps, tq)
    grid_spec = pltpu.PrefetchScalarGridSpec(
        num_scalar_prefetch=2,
        grid=(b, ng, t // tq),
        in_specs=[pl.BlockSpec((1, tq, hps * dh), lambda bi, g, i, *_: (bi, i, g)),
                  pl.BlockSpec((1, t, hps * dh), lambda bi, g, i, *_: (bi, 0, g)),
                  pl.BlockSpec((1, t, hps * dh), lambda bi, g, i, *_: (bi, 0, g)),
                  pl.BlockSpec((1, t // tq, 1, hps, tq), lambda bi, g, i, *_: (bi, 0, g, 0, 0)),
                  pl.BlockSpec((1, tq, LANES), lambda bi, g, i, *_: (bi, i, 0))],
        out_specs=pl.BlockSpec((1, tq, hps * dh), lambda bi, g, i, *_: (bi, i, g)),
        scratch_shapes=[pltpu.VMEM((hps, tq, LANES), F32), pltpu.VMEM((hps, tq, LANES), F32),
                        pltpu.VMEM((hps, tq, LANES), F32)],
    )
    return pl.pallas_call(
        functools.partial(_attn_kernel, tq=tq, dh=dh, nh=nh, hps=hps, online=online),
        out_shape=jax.ShapeDtypeStruct((b, t, nh * dh), BF16),
        grid_spec=grid_spec,
        compiler_params=pltpu.CompilerParams(
            dimension_semantics=("arbitrary", "arbitrary", "arbitrary"), vmem_limit_bytes=VMEM_LIMIT),
        name="fox_attn_online" if online else "fox_attn",
    )(cb, thr, qb, kb, vb, crow5, ccol)


def _sample_attn_kernel(q_ref, kn_ref, vn_ref, crn_ref, ccol_ref, kc_ref, vc_ref, clf_ref, o_ref, c_scr,
                        *, nh, dh, rb):
    s_len, p_len = q_ref.shape[1], kc_ref.shape[3]

    def cache_rows(r, carry):
        rows = pl.ds(pl.multiple_of(r * rb, rb), rb)
        c = _cumsum_rows(clf_ref[0, rows, :]) + carry
        c_scr[rows, :] = c
        return c[rb - 1:rb, :]

    c_end = lax.fori_loop(0, p_len // rb, cache_rows, jnp.zeros((1, LANES), F32))
    c_cache = ((c_scr[...] - c_end) * LOG2E).T[:nh, :]

    causal = (lax.broadcasted_iota(jnp.int32, (s_len, s_len), 0)
              >= lax.broadcasted_iota(jnp.int32, (s_len, s_len), 1))
    for hd in range(nh):
        cols = slice(dh * hd, dh * (hd + 1))
        q = q_ref[0, :, cols]
        cq = ccol_ref[0, :, hd:hd + 1]
        kc_t = kc_ref[0, hd].astype(BF16)
        vc_t = vc_ref[0, hd].astype(BF16)
        t1 = jnp.dot(q, kc_t, preferred_element_type=F32) - c_cache[hd:hd + 1, :] + cq
        t2 = lax.dot_general(q, kn_ref[0, :, cols], _NT, preferred_element_type=F32) - crn_ref[0, 0, hd:hd + 1, :] + cq
        t2 = jnp.where(causal, t2, NEG_BIG)
        m = jnp.maximum(jnp.max(t1, axis=-1, keepdims=True), jnp.max(t2, axis=-1, keepdims=True))
        p1 = jnp.exp2(t1 - m)
        p2 = jnp.exp2(t2 - m)
        den = jnp.sum(p1, axis=-1, keepdims=True) + jnp.sum(p2, axis=-1, keepdims=True)
        acc = (lax.dot_general(p1.astype(BF16), vc_t, _NT, preferred_element_type=F32)
               + jnp.dot(p2.astype(BF16), vn_ref[0, :, cols], preferred_element_type=F32))
        o_ref[0, :, cols] = (acc / den).astype(o_ref.dtype)


def _sample_attn(qb, kb_n, vb_n, crow_n, ccol, cache_kt, cache_vt, clf_pad, *, nh, dh):
    b, s_len, bw = qb.shape
    p_len = cache_kt.shape[3]
    whole = lambda bi: (bi, 0, 0)
    whole4 = lambda bi: (bi, 0, 0, 0)
    return pl.pallas_call(
        functools.partial(_sample_attn_kernel, nh=nh, dh=dh, rb=min(HGRN_CHUNK, p_len)),
        out_shape=jax.ShapeDtypeStruct((b, s_len, bw), BF16),
        grid=(b,),
        in_specs=[pl.BlockSpec((1, s_len, bw), whole), pl.BlockSpec((1, s_len, bw), whole),
                  pl.BlockSpec((1, s_len, bw), whole), pl.BlockSpec((1, 1, nh, s_len), whole4),
                  pl.BlockSpec((1, s_len, LANES), whole),
                  pl.BlockSpec((1, nh, dh, p_len), whole4), pl.BlockSpec((1, nh, dh, p_len), whole4),
                  pl.BlockSpec((1, p_len, LANES), whole)],
        out_specs=pl.BlockSpec((1, s_len, bw), whole),
        scratch_shapes=[pltpu.VMEM((p_len, LANES), F32)],
        compiler_params=pltpu.CompilerParams(dimension_semantics=("arbitrary",), vmem_limit_bytes=VMEM_LIMIT),
        name="fox_sample_attn",
    )(qb, kb_n, vb_n, crow_n, ccol, cache_kt, cache_vt, clf_pad)


def _ffn_kernel(x_ref, oa_ref, ob_ref, woa_ref, wob_ref, n2_ref, wup_ref, wdn_ref, y_ref, h_scr, *, ff_chunk):
    x1 = (x_ref[...]
          + jnp.dot(oa_ref[...], woa_ref[...], preferred_element_type=F32)
          + jnp.dot(ob_ref[...], wob_ref[...], preferred_element_type=F32))
    ms = jnp.mean(x1 * x1, axis=-1, keepdims=True)
    h_scr[...] = (x1 * lax.rsqrt(ms + EPS) * n2_ref[...]).astype(BF16)
    y_ref[...] = x1
    for c in range(wup_ref.shape[1] // ff_chunk):
        cs = slice(ff_chunk * c, ff_chunk * (c + 1))
        u = jnp.maximum(jnp.dot(h_scr[...], wup_ref[:, cs], preferred_element_type=F32), 0.0)
        y_ref[...] += jnp.dot((u * u).astype(BF16), wdn_ref[cs, :], preferred_element_type=F32)


def _ffn(x2d, oa2d, ob2d, woa, wob, n2, wup, wdn):
    n, d = x2d.shape
    tc = min(TOKEN_TILE, n)
    aw, bw = oa2d.shape[1], ob2d.shape[1]
    row = lambda i: (i, 0)
    return pl.pallas_call(
        functools.partial(_ffn_kernel, ff_chunk=min(1024, wup.shape[1])),
        out_shape=jax.ShapeDtypeStruct((n, d), F32),
        grid=(n // tc,),
        in_specs=[pl.BlockSpec((tc, d), row), pl.BlockSpec((tc, aw), row), pl.BlockSpec((tc, bw), row),
                  _const_spec(woa.shape), _const_spec(wob.shape), _const_spec(n2.shape),
                  _const_spec(wup.shape), _const_spec(wdn.shape)],
        out_specs=pl.BlockSpec((tc, d), row),
        scratch_shapes=[pltpu.VMEM((tc, d), BF16)],
        compiler_params=pltpu.CompilerParams(
            dimension_semantics=("arbitrary",), vmem_limit_bytes=VMEM_LIMIT),
        name="out_ffn",
    )(x2d, oa2d, ob2d, woa, wob, n2, wup, wdn)


def kernel(x_prompt, x_sample, cache_fox_k, cache_fox_v, cache_fox_logf, state_hgrn, norm1, w_in, b_fox_f,
           q_norm_gain, k_norm_gain, hgrn_lb_logits, hgrn_out_norm, w_out, norm2, w_up, w_down):
    depth = w_in.shape[0]
    d = x_prompt.shape[-1]
    nh, dh = cache_fox_k.shape[3], cache_fox_k.shape[4]
    nha, dk, dv = state_hgrn.shape[2], state_hgrn.shape[3], state_hgrn.shape[4]
    aw, bw = nha * dk, nh * dh
    assert dk == LANES and dv == LANES and 2 * dh == LANES and nh % 2 == 0 and nh <= 8
    assert w_in.shape[2] == 4 * aw + 3 * bw + nh and aw + bw == w_out.shape[1] and aw % MXU_COLS == 0

    xp, xs = x_prompt, x_sample
    bp, tp, _ = xp.shape
    bs, ts, _ = xs.shape
    plen = cache_fox_k.shape[2]
    lb_logits = hgrn_lb_logits.astype(F32)

    kp, vp, lp, sp = [], [], [], []
    ksl, vsl, lsl, ssl = [], [], [], []
    for l in range(depth):
        w_all = jnp.pad(w_in[l], ((0, 0), (0, MXU_COLS - nh))).astype(BF16)
        w_all = jnp.transpose(w_all.reshape(d, -1, MXU_COLS), (1, 0, 2))
        bf_pad = jnp.pad(b_fox_f[l].astype(F32), (0, LANES - nh)).reshape(1, LANES)
        qg2 = jnp.tile(q_norm_gain[l].astype(F32), 2).reshape(1, LANES)
        kg2 = jnp.tile(k_norm_gain[l].astype(F32), 2).reshape(1, LANES)
        n1 = norm1[l].astype(F32).reshape(1, d)
        n2 = norm2[l].astype(F32).reshape(1, d)
        gnorm = hgrn_out_norm[l].astype(F32).reshape(1, aw)
        woa = w_out[l, :aw].astype(BF16)
        wob = w_out[l, aw:].astype(BF16)
        wup = w_up[l].astype(BF16)
        wdn = w_down[l].astype(BF16)
        mixer_in = functools.partial(_mixer_in, layer=l, nh=nh, dh=dh)

        (k_p, v_p, lf_p, qb, kb, vb, ccol, crow, cends, oa, s_p) = mixer_in(
            xp, n1, w_all, bf_pad, qg2, kg2, lb_logits, gnorm, jnp.zeros((bp, nha, dk, dv), F32),
            chunk=min(HGRN_CHUNK, tp))
        per_blk = cends.shape[1] // (tp // min(TOKEN_TILE, tp))
        cb = jnp.transpose(cends[:, per_blk - 1::per_blk, :nh], (0, 2, 1)).reshape(-1)
        qk_bound = dh ** 0.5 * jnp.max(jnp.abs(q_norm_gain[l])) * jnp.max(jnp.abs(k_norm_gain[l]))
        thr = (2.04 * LOG2E * qk_bound + 2.0).astype(F32).reshape(1)
        ob = lax.cond(thr[0] <= 2.0 * FIXED_OFFSET_MAX_LOGIT,
                      functools.partial(_attn, nh=nh, dh=dh, online=False),
                      functools.partial(_attn, nh=nh, dh=dh, online=True),
                      cb, thr, qb, kb, vb, crow, ccol)
        xp = _ffn(xp.reshape(bp * tp, d), oa.reshape(bp * tp, aw), ob.reshape(bp * tp, bw),
                  woa, wob, n2, wup, wdn).reshape(bp, tp, d)
        kp.append(k_p.reshape(bp, tp, nh, dh))
        vp.append(v_p.reshape(bp, tp, nh, dh))
        lp.append(jnp.transpose(lf_p, (0, 2, 1)))
        sp.append(s_p.astype(x_prompt.dtype))

        clf_pad = jnp.pad(cache_fox_logf[l].astype(F32), ((0, 0), (0, 0), (0, LANES - nh)))
        (k_s, v_s, lf_s, qb, kb, vb, ccol, crow, _, oa, s_s) = mixer_in(
            xs, n1, w_all, bf_pad, qg2, kg2, lb_logits, gnorm, state_hgrn[l].astype(F32), chunk=ts)
        ob = _sample_attn(qb, kb, vb, crow, ccol, jnp.transpose(cache_fox_k[l].astype(F32), (0, 2, 3, 1)),
                          jnp.transpose(cache_fox_v[l].astype(F32), (0, 2, 3, 1)), clf_pad, nh=nh, dh=dh)
        xs = _ffn(xs.reshape(bs * ts, d), oa.reshape(bs * ts, aw), ob.reshape(bs * ts, bw),
                  woa, wob, n2, wup, wdn).reshape(bs, ts, d)
        ksl.append(k_s.reshape(bs, ts, nh, dh))
        vsl.append(v_s.reshape(bs, ts, nh, dh))
        lsl.append(jnp.transpose(lf_s, (0, 2, 1)))
        ssl.append(s_s.astype(state_hgrn.dtype))

    return (xp, xs, jnp.stack(kp), jnp.stack(vp), jnp.stack(lp), jnp.stack(sp),
            jnp.stack(ksl), jnp.stack(vsl), jnp.stack(lsl), jnp.stack(ssl))
```

```python
import functools

import jax
import jax.numpy as jnp
from jax import lax
from jax.experimental import pallas as pl
from jax.experimental.pallas import tpu as pltpu

F32 = jnp.float32
BF16 = jnp.bfloat16

EPS = 1e-6
HGRN_CHUNK = 64
TOKEN_TILE = 512
ATTN_HEADS_PER_STEP = 8
LANES = 128
MXU_COLS = 256
NEG_BIG = -1e30
LOG2E = 1.4426950408889634
EXP2_ZERO_BELOW = -150.0
FIXED_OFFSET_MAX_LOGIT = 60.0
VMEM_LIMIT = 56 * 1024 * 1024

_NT = (((1,), (1,)), ((), ()))
_TN = (((0,), (0,)), ((), ()))


def _const_spec(shape):
    nd = len(shape)
    return pl.BlockSpec(shape, lambda *_: (0,) * nd, pipeline_mode=pl.Buffered(1))


def _cumsum_rows(x):
    n = x.shape[0]
    row = lax.broadcasted_iota(jnp.int32, x.shape, 0)
    s = 1
    while s < n:
        x = x + jnp.where(row >= s, pltpu.roll(x, s, axis=0), 0.0)
        s *= 2
    return x


def _pair_rms_norm(z, gain2, dh):
    lane = lax.broadcasted_iota(jnp.int32, z.shape, 1)
    first = lane < dh
    sq = z * z
    s0 = jnp.sum(jnp.where(first, sq, 0.0), axis=-1, keepdims=True)
    s1 = jnp.sum(jnp.where(first, 0.0, sq), axis=-1, keepdims=True)
    ms = jnp.where(first, s0, s1) * (1.0 / dh)
    return z * lax.rsqrt(ms + EPS) * gain2


def _log_sigmoid(x):
    return jnp.minimum(x, 0.0) - jnp.log(1.0 + jnp.exp(-jnp.abs(x)))


def _hgrn_chunk(q, fa, ia, ga, lb, gn, st_ref, hd, causal):
    chunk = q.shape[0]
    f = lb + (1.0 - lb) * jax.nn.sigmoid(fa)
    k = 1.0 - f
    b = _cumsum_rows(jnp.log(f))
    b_ref = b[chunk // 2:chunk // 2 + 1, :]
    b_last = b[chunk - 1:chunk, :]
    v = ia.astype(BF16)
    st = st_ref[hd]
    inter = lax.dot_general((q * jnp.exp(b)).astype(BF16), st.astype(BF16), _NT, preferred_element_type=F32)
    qr = (q * jnp.exp(b - b_ref)).astype(BF16)
    kr = (k * jnp.exp(b_ref - b)).astype(BF16)
    a = lax.dot_general(qr, kr, _NT, preferred_element_type=F32)
    a = jnp.where(causal, a, 0.0).astype(BF16)
    o = inter + jnp.dot(a, v, preferred_element_type=F32)
    k3 = (k * jnp.exp(b_last - b)).astype(BF16)
    st_ref[hd] = st * jnp.exp(b_last) + lax.dot_general(v, k3, _TN, preferred_element_type=F32)
    ms = jnp.mean(o * o, axis=-1, keepdims=True)
    return (o * lax.rsqrt(ms + EPS) * gn * (ga * jax.nn.sigmoid(ga))).astype(BF16)


def _mixer_in_kernel(x_ref, n1_ref, wf_ref, wh_ref, bf_ref, qg_ref, kg_ref, lbl_ref, gn_ref, s0_ref,
                     k_out, v_out, lf_out, qb_out, kb_out, vb_out, ccol_out, crow_out, cends_out,
                     oa_out, s_out,
                     h_scr, zf_scr, zh_even, zh_odd, c_scr, lf_scr, carry_scr, st_scr,
                     *, layer, aw, bw, nh, dh, nha, rb, tiles_per_seq, n_tiles):
    t = pl.program_id(0)
    ip = jnp.minimum(t, n_tiles - 1) % tiles_per_seq
    ih = jnp.maximum(t - 1, 0) % tiles_per_seq

    @pl.when(t == 0)
    def _():
        zh_odd[...] = jnp.zeros(zh_odd.shape, F32)
        st_scr[...] = jnp.zeros(st_scr.shape, F32)

    @pl.when(jnp.logical_and(ip == 0, t < n_tiles))
    def _():
        carry_scr[...] = jnp.zeros(carry_scr.shape, F32)

    @pl.when(jnp.logical_and(ih == 0, t >= 1))
    def _():
        for hd in range(nha):
            st_scr[hd] = s0_ref[0, hd].T

    x = x_ref[0]
    ta = x.shape[0]
    nrb = ta // rb
    nsl = wh_ref.shape[0]
    ms = jnp.mean(x * x, axis=-1, keepdims=True)
    h_scr[...] = (x * lax.rsqrt(ms + EPS) * n1_ref[...]).astype(BF16)
    zf_scr[...] = jnp.dot(h_scr[...], wf_ref[...], preferred_element_type=F32)

    interleave = nrb == nsl

    lg = lbl_ref[...]
    e = jnp.exp(lg - jnp.max(lg, axis=0, keepdims=True))
    lb = jnp.sum(e[:layer + 1], axis=0, keepdims=True) / jnp.sum(e, axis=0, keepdims=True)

    scale = dh ** -0.5 * LOG2E
    rb_idx = lax.broadcasted_iota(jnp.int32, (nrb, LANES), 0)
    causal = (lax.broadcasted_iota(jnp.int32, (rb, rb), 0) >= lax.broadcasted_iota(jnp.int32, (rb, rb), 1))
    spw = aw // MXU_COLS
    hps = MXU_COLS // LANES

    def trip(zh_new, zh_old, r, state):
        carry, cends = state
        rows = pl.ds(pl.multiple_of(r * rb, rb), rb)
        if interleave:
            zh_new[r] = jnp.dot(h_scr[...], wh_ref[r], preferred_element_type=F32)

        lf = _log_sigmoid(zf_scr[rows, 3 * bw:3 * bw + LANES] + bf_ref[...])
        lf_scr[rows, :] = lf
        c = _cumsum_rows(lf) + carry
        c2 = c * LOG2E
        c_scr[rows, :] = c2
        ccol_out[0, rows, :] = c2
        for p in range(nh // 2):
            cols = slice(LANES * p, LANES * (p + 1))
            qn = _pair_rms_norm(zf_scr[rows, LANES * p:LANES * (p + 1)], qg_ref[...], dh) * scale
            kn = _pair_rms_norm(zf_scr[rows, bw + LANES * p:bw + LANES * (p + 1)], kg_ref[...], dh)
            vv = zf_scr[rows, 2 * bw + LANES * p:2 * bw + LANES * (p + 1)]
            qb_out[0, rows, cols] = qn.astype(BF16)
            kb_out[0, rows, cols] = kn.astype(BF16)
            vb_out[0, rows, cols] = vv.astype(BF16)
            k_out[0, rows, cols] = kn
            v_out[0, rows, cols] = vv

        for hd in range(nha):
            sl, ln = hd // hps, slice(LANES * (hd % hps), LANES * (hd % hps + 1))
            cols = slice(LANES * hd, LANES * (hd + 1))
            oa_out[0, rows, cols] = _hgrn_chunk(
                zh_old[sl, rows, ln], zh_old[spw + sl, rows, ln],
                zh_old[2 * spw + sl, rows, ln], zh_old[3 * spw + sl, rows, ln],
                lb[:, cols], gn_ref[:, cols], st_scr, hd, causal)

        carry = c[rb - 1:rb, :]
        return carry, jnp.where(rb_idx == r, carry, cends)

    def run(zh_new, zh_old):
        if not interleave:
            for sl in range(nsl):
                zh_new[sl] = jnp.dot(h_scr[...], wh_ref[sl], preferred_element_type=F32)
        carry, cends = lax.fori_loop(0, nrb, functools.partial(trip, zh_new, zh_old),
                                     (carry_scr[...], jnp.zeros((nrb, LANES), F32)),
                                     unroll=max(u for u in (8, 4, 2, 1) if nrb % u == 0))
        cends_out[0] = cends

        @pl.when(t < n_tiles - 1)
        def _():
            carry_scr[...] = carry

    pl.when(t % 2 == 0)(lambda: run(zh_even, zh_odd))
    pl.when(t % 2 == 1)(lambda: run(zh_odd, zh_even))
    crow_out[0, 0] = c_scr[...].T[:nh, :]
    lf_out[0] = lf_scr[...].T[:nh, :]

    @pl.when(ih == tiles_per_seq - 1)
    def _():
        for hd in range(nha):
            s_out[0, hd] = st_scr[hd].T


def _mixer_in(x, n1, w_fox, w_hgrn, bf_pad, qg2, kg2, lb_logits, gnorm, s0, *, layer, nh, dh, chunk):
    b, t, d = x.shape
    nha, dk, dv = s0.shape[1], s0.shape[2], s0.shape[3]
    aw, bw = nha * dk, nh * dh
    ta = min(TOKEN_TILE, t)
    rb = chunk
    tps = t // ta
    n_tiles = b * tps
    assert ta % rb == 0 and t % ta == 0

    def proj_side(*tail):
        return lambda s: (jnp.minimum(s, n_tiles - 1) // tps, jnp.minimum(s, n_tiles - 1) % tps) + tail

    def rec_side(*tail):
        return lambda s: (jnp.maximum(s - 1, 0) // tps, jnp.maximum(s - 1, 0) % tps) + tail

    rec_seq = lambda s: (jnp.maximum(s - 1, 0) // tps, 0, 0, 0)
    tok = proj_side(0)
    out_shape = (
        jax.ShapeDtypeStruct((b, t, bw), F32),
        jax.ShapeDtypeStruct((b, t, bw), F32),
        jax.ShapeDtypeStruct((b, nh, t), F32),
        jax.ShapeDtypeStruct((b, t, bw), BF16),
        jax.ShapeDtypeStruct((b, t, bw), BF16),
        jax.ShapeDtypeStruct((b, t, bw), BF16),
        jax.ShapeDtypeStruct((b, t, LANES), F32),
        jax.ShapeDtypeStruct((b, tps, nh, ta), F32),
        jax.ShapeDtypeStruct((b, t // rb, LANES), F32),
        jax.ShapeDtypeStruct((b, t, aw), BF16),
        jax.ShapeDtypeStruct((b, nha, dk, dv), F32),
    )
    out_specs = (
        pl.BlockSpec((1, ta, bw), tok), pl.BlockSpec((1, ta, bw), tok),
        pl.BlockSpec((1, nh, ta), lambda s: (jnp.minimum(s, n_tiles - 1) // tps, 0, jnp.minimum(s, n_tiles - 1) % tps)),
        pl.BlockSpec((1, ta, bw), tok), pl.BlockSpec((1, ta, bw), tok), pl.BlockSpec((1, ta, bw), tok),
        pl.BlockSpec((1, ta, LANES), tok),
        pl.BlockSpec((1, 1, nh, ta), proj_side(0, 0)),
        pl.BlockSpec((1, ta // rb, LANES), tok),
        pl.BlockSpec((1, ta, aw), rec_side(0)),
        pl.BlockSpec((1, nha, dk, dv), rec_seq),
    )
    in_specs = [
        pl.BlockSpec((1, ta, d), tok),
        _const_spec(n1.shape), _const_spec(w_fox.shape), _const_spec(w_hgrn.shape), _const_spec(bf_pad.shape),
        _const_spec(qg2.shape), _const_spec(kg2.shape), _const_spec(lb_logits.shape), _const_spec(gnorm.shape),
        pl.BlockSpec((1, nha, dk, dv), rec_seq),
    ]
    nsl = w_hgrn.shape[0]
    return pl.pallas_call(
        functools.partial(_mixer_in_kernel, layer=layer, aw=aw, bw=bw, nh=nh, dh=dh, nha=nha, rb=rb,
                          tiles_per_seq=tps, n_tiles=n_tiles),
        out_shape=out_shape,
        grid=(n_tiles + 1,),
        in_specs=in_specs,
        out_specs=out_specs,
        scratch_shapes=[pltpu.VMEM((ta, d), BF16), pltpu.VMEM((ta, w_fox.shape[1]), F32),
                        pltpu.VMEM((nsl, ta, MXU_COLS), F32), pltpu.VMEM((nsl, ta, MXU_COLS), F32),
                        pltpu.VMEM((ta, LANES), F32), pltpu.VMEM((ta, LANES), F32),
                        pltpu.VMEM((1, LANES), F32), pltpu.VMEM((nha, dv, dk), F32)],
        compiler_params=pltpu.CompilerParams(dimension_semantics=("arbitrary",), vmem_limit_bytes=VMEM_LIMIT),
        name="mixer_in",
    )(x, n1, w_fox, w_hgrn, bf_pad, qg2, kg2, lb_logits, gnorm, s0)


def _one_head(q_pair, odd, dh):
    lane = lax.broadcasted_iota(jnp.int32, q_pair.shape, 1)
    keep = lane >= dh if odd else lane < dh
    return jnp.where(keep, q_pair, jnp.zeros_like(q_pair))


def _head_column(c_block, lane_idx):
    lane = lax.broadcasted_iota(jnp.int32, c_block.shape, 1)
    col = jnp.sum(jnp.where(lane == lane_idx, c_block, 0.0), axis=-1, keepdims=True)
    return jnp.broadcast_to(col, c_block.shape)


def _flash_step(q, k, v, ck, cq, m_ref, l_ref, acc_ref, hh, rows, mask):
    t = lax.dot_general(q, k, _NT, preferred_element_type=F32) - ck
    if mask is not None:
        t = jnp.where(mask, t, NEG_BIG)
    m_prev = m_ref[hh, rows, :]
    m_new = jnp.maximum(m_prev, jnp.max(t, axis=-1, keepdims=True) + cq)
    alpha = jnp.exp2(m_prev - m_new)
    off = m_new - cq
    ps = [jnp.exp2(t[:, LANES * c:LANES * (c + 1)] - off) for c in range(t.shape[1] // LANES)]
    l_ref[hh, rows, :] = alpha * l_ref[hh, rows, :] + sum(ps[1:], ps[0])
    p = jnp.concatenate(ps, axis=1).astype(BF16)
    acc_ref[hh, rows, :] = alpha * acc_ref[hh, rows, :] + jnp.dot(p, v, preferred_element_type=F32)
    m_ref[hh, rows, :] = m_new


def _fixed_offset_step(q, k, v, ck, cq, l_ref, acc_ref, hh, rows, mask):
    s = lax.dot_general(q, k, _NT, preferred_element_type=F32)
    ps = []
    for c in range(s.shape[1] // LANES):
        cols = slice(LANES * c, LANES * (c + 1))
        t = s[:, cols] - ck[:, cols] + cq
        if mask is not None:
            t = jnp.where(mask[:, cols], t, NEG_BIG)
        ps.append(jnp.exp2(t))
    l_ref[hh, rows, :] += sum(ps[1:], ps[0])
    p = jnp.concatenate(ps, axis=1).astype(BF16)
    acc_ref[hh, rows, :] += jnp.dot(p, v, preferred_element_type=F32)


def _merge_pair(acc0, l0, acc1, l1, dh):
    o0 = acc0 / jnp.sum(l0, axis=-1, keepdims=True)
    o1 = acc1 / jnp.sum(l1, axis=-1, keepdims=True)
    lane = lax.broadcasted_iota(jnp.int32, o0.shape, 1)
    return jnp.where(lane < dh, o0, o1)


def _attn_kernel(cb_ref, thr_ref, q_ref, k_ref, v_ref, crow_ref, ccol_ref, o_ref, m_scr, l_scr, acc_scr,
                 *, tq, dh, nh, hps, online):
    bi, g, i = pl.program_id(0), pl.program_id(1), pl.program_id(2)
    nblk = pl.num_programs(2)
    acc_scr[...] = jnp.zeros(acc_scr.shape, F32)
    l_scr[...] = jnp.zeros(l_scr.shape, F32)
    if online:
        m_scr[...] = jnp.full(m_scr.shape, NEG_BIG, F32)

    def step(q, k, v, ck, cq, hh, rows, mask):
        if online:
            _flash_step(q, k, v, ck, cq, m_scr, l_scr, acc_scr, hh, rows, mask)
        else:
            _fixed_offset_step(q, k, v, ck, cq, l_scr, acc_scr, hh, rows, mask)

    def first_live_block(hd):
        base = (bi * nh + hd) * nblk
        c_before = cb_ref[base + jnp.maximum(i - 1, 0)]

        def previous_is_live(n):
            c_end = cb_ref[base + jnp.maximum(n - 1, 0)]
            return jnp.logical_and(n > 0, (c_before - c_end) * LOG2E + thr_ref[0] >= EXP2_ZERO_BELOW)

        return lax.while_loop(previous_is_live, lambda n: n - 1, i)

    j_lo = first_live_block(hps * g)
    for hh in range(1, hps):
        j_lo = jnp.minimum(j_lo, first_live_block(hps * g + hh))

    def q_of(hh, rows=slice(None)):
        return _one_head(q_ref[0, rows, LANES * (hh // 2):LANES * (hh // 2 + 1)], hh % 2, dh)

    def cq_of(hh, rows=slice(None)):
        return _head_column(ccol_ref[0, rows, :], hps * g + hh)

    def body(j, _):
        keys = pl.ds(pl.multiple_of(j * tq, tq), tq)
        for hh in range(hps):
            kcols = slice(LANES * (hh // 2), LANES * (hh // 2 + 1))
            step(q_of(hh), k_ref[0, keys, kcols], v_ref[0, keys, kcols], crow_ref[0, j, 0, hh:hh + 1, :],
                 cq_of(hh), hh, slice(0, tq), None)
        return 0

    lax.fori_loop(j_lo, i, body, 0)

    half = tq // 2
    keys_a = pl.ds(pl.multiple_of(i * tq, tq), half)
    keys_b = pl.ds(pl.multiple_of(i * tq + half, half), half)
    mask_a = (lax.broadcasted_iota(jnp.int32, (tq, half), 0) >= lax.broadcasted_iota(jnp.int32, (tq, half), 1))
    mask_b = (lax.broadcasted_iota(jnp.int32, (half, half), 0) >= lax.broadcasted_iota(jnp.int32, (half, half), 1))
    late = slice(half, tq)
    for hh in range(hps):
        kcols = slice(LANES * (hh // 2), LANES * (hh // 2 + 1))
        step(q_of(hh), k_ref[0, keys_a, kcols], v_ref[0, keys_a, kcols], crow_ref[0, i, 0, hh:hh + 1, :half],
             cq_of(hh), hh, slice(0, tq), mask_a)
        step(q_of(hh, late), k_ref[0, keys_b, kcols], v_ref[0, keys_b, kcols],
             crow_ref[0, i, 0, hh:hh + 1, half:], cq_of(hh, late), hh, late, mask_b)
    for p in range(hps // 2):
        o_ref[0, :, LANES * p:LANES * (p + 1)] = _merge_pair(
            acc_scr[2 * p], l_scr[2 * p], acc_scr[2 * p + 1], l_scr[2 * p + 1], dh).astype(o_ref.dtype)


def _attn(cb, thr, qb, kb, vb, crow, ccol, *, nh, dh, online):
    b, t, _ = qb.shape
    tq = min(TOKEN_TILE, t)
    hps = min(ATTN_HEADS_PER_STEP, nh)
    ng = nh // hps
    crow5 = crow.reshape(b, t // tq, ng, hps, tq)
    grid_spec = pltpu.PrefetchScalarGridSpec(
        num_scalar_prefetch=2,
        grid=(b, ng, t // tq),
        in_specs=[pl.BlockSpec((1, tq, hps * dh), lambda bi, g, i, *_: (bi, i, g)),
                  pl.BlockSpec((1, t, hps * dh), lambda bi, g, i, *_: (bi, 0, g)),
                  pl.BlockSpec((1, t, hps * dh), lambda bi, g, i, *_: (bi, 0, g)),
                  pl.BlockSpec((1, t // tq, 1, hps, tq), lambda bi, g, i, *_: (bi, 0, g, 0, 0)),
                  pl.BlockSpec((1, tq, LANES), lambda bi, g, i, *_: (bi, i, 0))],
        out_specs=pl.BlockSpec((1, tq, hps * dh), lambda bi, g, i, *_: (bi, i, g)),
        scratch_shapes=[pltpu.VMEM((hps, tq, LANES), F32), pltpu.VMEM((hps, tq, LANES), F32),
                        pltpu.VMEM((hps, tq, LANES), F32)],
    )
    return pl.pallas_call(
        functools.partial(_attn_kernel, tq=tq, dh=dh, nh=nh, hps=hps, online=online),
        out_shape=jax.ShapeDtypeStruct((b, t, nh * dh), BF16),
        grid_spec=grid_spec,
        compiler_params=pltpu.CompilerParams(
            dimension_semantics=("arbitrary", "arbitrary", "arbitrary"), vmem_limit_bytes=VMEM_LIMIT),
        name="fox_attn_online" if online else "fox_attn",
    )(cb, thr, qb, kb, vb, crow5, ccol)


def _sample_attn_kernel(q_ref, kn_ref, vn_ref, crn_ref, ccol_ref, kc_ref, vc_ref, clf_ref, o_ref, c_scr,
                        *, nh, dh, rb):
    s_len, p_len = q_ref.shape[1], kc_ref.shape[3]

    def cache_rows(r, carry):
        rows = pl.ds(pl.multiple_of(r * rb, rb), rb)
        c = _cumsum_rows(clf_ref[0, rows, :]) + carry
        c_scr[rows, :] = c
        return c[rb - 1:rb, :]

    c_end = lax.fori_loop(0, p_len // rb, cache_rows, jnp.zeros((1, LANES), F32))
    c_cache = ((c_scr[...] - c_end) * LOG2E).T[:nh, :]

    causal = (lax.broadcasted_iota(jnp.int32, (s_len, s_len), 0)
              >= lax.broadcasted_iota(jnp.int32, (s_len, s_len), 1))
    for hd in range(nh):
        cols = slice(dh * hd, dh * (hd + 1))
        q = q_ref[0, :, cols]
        cq = ccol_ref[0, :, hd:hd + 1]
        kc_t = kc_ref[0, hd].astype(BF16)
        vc_t = vc_ref[0, hd].astype(BF16)
        t1 = jnp.dot(q, kc_t, preferred_element_type=F32) - c_cache[hd:hd + 1, :] + cq
        t2 = lax.dot_general(q, kn_ref[0, :, cols], _NT, preferred_element_type=F32) - crn_ref[0, 0, hd:hd + 1, :] + cq
        t2 = jnp.where(causal, t2, NEG_BIG)
        m = jnp.maximum(jnp.max(t1, axis=-1, keepdims=True), jnp.max(t2, axis=-1, keepdims=True))
        p1 = jnp.exp2(t1 - m)
        p2 = jnp.exp2(t2 - m)
        den = jnp.sum(p1, axis=-1, keepdims=True) + jnp.sum(p2, axis=-1, keepdims=True)
        acc = (lax.dot_general(p1.astype(BF16), vc_t, _NT, preferred_element_type=F32)
               + jnp.dot(p2.astype(BF16), vn_ref[0, :, cols], preferred_element_type=F32))
        o_ref[0, :, cols] = (acc / den).astype(o_ref.dtype)


def _sample_attn(qb, kb_n, vb_n, crow_n, ccol, cache_kt, cache_vt, clf_pad, *, nh, dh):
    b, s_len, bw = qb.shape
    p_len = cache_kt.shape[3]
    whole = lambda bi: (bi, 0, 0)
    whole4 = lambda bi: (bi, 0, 0, 0)
    return pl.pallas_call(
        functools.partial(_sample_attn_kernel, nh=nh, dh=dh, rb=min(HGRN_CHUNK, p_len)),
        out_shape=jax.ShapeDtypeStruct((b, s_len, bw), BF16),
        grid=(b,),
        in_specs=[pl.BlockSpec((1, s_len, bw), whole), pl.BlockSpec((1, s_len, bw), whole),
                  pl.BlockSpec((1, s_len, bw), whole), pl.BlockSpec((1, 1, nh, s_len), whole4),
                  pl.BlockSpec((1, s_len, LANES), whole),
                  pl.BlockSpec((1, nh, dh, p_len), whole4), pl.BlockSpec((1, nh, dh, p_len), whole4),
                  pl.BlockSpec((1, p_len, LANES), whole)],
        out_specs=pl.BlockSpec((1, s_len, bw), whole),
        scratch_shapes=[pltpu.VMEM((p_len, LANES), F32)],
        compiler_params=pltpu.CompilerParams(dimension_semantics=("arbitrary",), vmem_limit_bytes=VMEM_LIMIT),
        name="fox_sample_attn",
    )(qb, kb_n, vb_n, crow_n, ccol, cache_kt, cache_vt, clf_pad)


def _ffn_kernel(x_ref, oa_ref, ob_ref, woa_ref, wob_ref, n2_ref, wup_ref, wdn_ref, y_ref, h_scr, *, ff_chunk):
    x1 = (x_ref[...]
          + jnp.dot(oa_ref[...], woa_ref[...], preferred_element_type=F32)
          + jnp.dot(ob_ref[...], wob_ref[...], preferred_element_type=F32))
    ms = jnp.mean(x1 * x1, axis=-1, keepdims=True)
    h_scr[...] = (x1 * lax.rsqrt(ms + EPS) * n2_ref[...]).astype(BF16)
    y_ref[...] = x1
    for c in range(wup_ref.shape[1] // ff_chunk):
        cs = slice(ff_chunk * c, ff_chunk * (c + 1))
        u = jnp.maximum(jnp.dot(h_scr[...], wup_ref[:, cs], preferred_element_type=F32), 0.0)
        y_ref[...] += jnp.dot((u * u).astype(BF16), wdn_ref[cs, :], preferred_element_type=F32)


def _ffn(x2d, oa2d, ob2d, woa, wob, n2, wup, wdn):
    n, d = x2d.shape
    tc = min(TOKEN_TILE, n)
    aw, bw = oa2d.shape[1], ob2d.shape[1]
    row = lambda i: (i, 0)
    return pl.pallas_call(
        functools.partial(_ffn_kernel, ff_chunk=min(1024, wup.shape[1])),
        out_shape=jax.ShapeDtypeStruct((n, d), F32),
        grid=(n // tc,),
        in_specs=[pl.BlockSpec((tc, d), row), pl.BlockSpec((tc, aw), row), pl.BlockSpec((tc, bw), row),
                  _const_spec(woa.shape), _const_spec(wob.shape), _const_spec(n2.shape),
                  _const_spec(wup.shape), _const_spec(wdn.shape)],
        out_specs=pl.BlockSpec((tc, d), row),
        scratch_shapes=[pltpu.VMEM((tc, d), BF16)],
        compiler_params=pltpu.CompilerParams(
            dimension_semantics=("arbitrary",), vmem_limit_bytes=VMEM_LIMIT),
        name="out_ffn",
    )(x2d, oa2d, ob2d, woa, wob, n2, wup, wdn)


def kernel(x_prompt, x_sample, cache_fox_k, cache_fox_v, cache_fox_logf, state_hgrn, norm1, w_in, b_fox_f,
           q_norm_gain, k_norm_gain, hgrn_lb_logits, hgrn_out_norm, w_out, norm2, w_up, w_down):
    depth = w_in.shape[0]
    d = x_prompt.shape[-1]
    nh, dh = cache_fox_k.shape[3], cache_fox_k.shape[4]
    nha, dk, dv = state_hgrn.shape[2], state_hgrn.shape[3], state_hgrn.shape[4]
    aw, bw = nha * dk, nh * dh
    assert dk == LANES and dv == LANES and 2 * dh == LANES and nh % 2 == 0 and nh <= 8
    assert w_in.shape[2] == 4 * aw + 3 * bw + nh and aw + bw == w_out.shape[1] and aw % MXU_COLS == 0

    xp, xs = x_prompt, x_sample
    bp, tp, _ = xp.shape
    bs, ts, _ = xs.shape
    lb_logits = hgrn_lb_logits.astype(F32)

    kp, vp, lp, sp = [], [], [], []
    ksl, vsl, lsl, ssl = [], [], [], []
    for l in range(depth):
        w_fox = jnp.pad(w_in[l, :, 4 * aw:], ((0, 0), (0, LANES - nh))).astype(BF16)
        w_hgrn = jnp.transpose(w_in[l, :, :4 * aw].reshape(d, 4 * aw // MXU_COLS, MXU_COLS), (1, 0, 2)).astype(BF16)
        bf_pad = jnp.pad(b_fox_f[l].astype(F32), (0, LANES - nh)).reshape(1, LANES)
        qg2 = jnp.tile(q_norm_gain[l].astype(F32), 2).reshape(1, LANES)
        kg2 = jnp.tile(k_norm_gain[l].astype(F32), 2).reshape(1, LANES)
        n1 = norm1[l].astype(F32).reshape(1, d)
        n2 = norm2[l].astype(F32).reshape(1, d)
        gnorm = hgrn_out_norm[l].astype(F32).reshape(1, aw)
        woa = w_out[l, :aw].astype(BF16)
        wob = w_out[l, aw:].astype(BF16)
        wup = w_up[l].astype(BF16)
        wdn = w_down[l].astype(BF16)
        mixer_in = functools.partial(_mixer_in, layer=l, nh=nh, dh=dh)

        (k_p, v_p, lf_p, qb, kb, vb, ccol, crow, cends, oa, s_p) = mixer_in(
            xp, n1, w_fox, w_hgrn, bf_pad, qg2, kg2, lb_logits, gnorm, jnp.zeros((bp, nha, dk, dv), F32),
            chunk=min(HGRN_CHUNK, tp))
        per_blk = cends.shape[1] // (tp // min(TOKEN_TILE, tp))
        cb = jnp.transpose(cends[:, per_blk - 1::per_blk, :nh], (0, 2, 1)).reshape(-1)
        qk_bound = dh ** 0.5 * jnp.max(jnp.abs(q_norm_gain[l])) * jnp.max(jnp.abs(k_norm_gain[l]))
        thr = (2.04 * LOG2E * qk_bound + 2.0).astype(F32).reshape(1)
        ob = lax.cond(thr[0] <= 2.0 * FIXED_OFFSET_MAX_LOGIT,
                      functools.partial(_attn, nh=nh, dh=dh, online=False),
                      functools.partial(_attn, nh=nh, dh=dh, online=True),
                      cb, thr, qb, kb, vb, crow, ccol)
        xp = _ffn(xp.reshape(bp * tp, d), oa.reshape(bp * tp, aw), ob.reshape(bp * tp, bw),
                  woa, wob, n2, wup, wdn).reshape(bp, tp, d)
        kp.append(k_p.reshape(bp, tp, nh, dh))
        vp.append(v_p.reshape(bp, tp, nh, dh))
        lp.append(jnp.transpose(lf_p, (0, 2, 1)))
        sp.append(s_p.astype(x_prompt.dtype))

        clf_pad = jnp.pad(cache_fox_logf[l].astype(F32), ((0, 0), (0, 0), (0, LANES - nh)))
        (k_s, v_s, lf_s, qb, kb, vb, ccol, crow, _, oa, s_s) = mixer_in(
            xs, n1, w_fox, w_hgrn, bf_pad, qg2, kg2, lb_logits, gnorm, state_hgrn[l].astype(F32), chunk=ts)
        ob = _sample_attn(qb, kb, vb, crow, ccol, jnp.transpose(cache_fox_k[l].astype(F32), (0, 2, 3, 1)),
                          jnp.transpose(cache_fox_v[l].astype(F32), (0, 2, 3, 1)), clf_pad, nh=nh, dh=dh)
        xs = _ffn(xs.reshape(bs * ts, d), oa.reshape(bs * ts, aw), ob.reshape(bs * ts, bw),
                  woa, wob, n2, wup, wdn).reshape(bs, ts, d)
        ksl.append(k_s.reshape(bs, ts, nh, dh))
        vsl.append(v_s.reshape(bs, ts, nh, dh))
        lsl.append(jnp.transpose(lf_s, (0, 2, 1)))
        ssl.append(s_s.astype(state_hgrn.dtype))

    return (xp, xs, jnp.stack(kp), jnp.stack(vp), jnp.stack(lp), jnp.stack(sp),
            jnp.stack(ksl), jnp.stack(vsl), jnp.stack(lsl), jnp.stack(ssl))
```

```python
import functools

import jax
import jax.numpy as jnp
from jax import lax
from jax.experimental import pallas as pl
from jax.experimental.pallas import tpu as pltpu

F32 = jnp.float32
BF16 = jnp.bfloat16

EPS = 1e-6
HGRN_CHUNK = 64
TOKEN_TILE = 512
ATTN_HEADS_PER_STEP = 8
LANES = 128
MXU_COLS = 256
NEG_BIG = -1e30
LOG2E = 1.4426950408889634
EXP2_ZERO_BELOW = -150.0
FIXED_OFFSET_MAX_LOGIT = 60.0
VMEM_LIMIT = 56 * 1024 * 1024

_NT = (((1,), (1,)), ((), ()))
_TN = (((0,), (0,)), ((), ()))


def _const_spec(shape):
    nd = len(shape)
    return pl.BlockSpec(shape, lambda *_: (0,) * nd, pipeline_mode=pl.Buffered(1))


def _cumsum_rows(x):
    n = x.shape[0]
    row = lax.broadcasted_iota(jnp.int32, x.shape, 0)
    s = 1
    while s < n:
        x = x + jnp.where(row >= s, pltpu.roll(x, s, axis=0), 0.0)
        s *= 2
    return x


def _pair_rms_norm(z, gain2, dh):
    lane = lax.broadcasted_iota(jnp.int32, z.shape, 1)
    first = lane < dh
    sq = z * z
    s0 = jnp.sum(jnp.where(first, sq, 0.0), axis=-1, keepdims=True)
    s1 = jnp.sum(jnp.where(first, 0.0, sq), axis=-1, keepdims=True)
    ms = jnp.where(first, s0, s1) * (1.0 / dh)
    return z * lax.rsqrt(ms + EPS) * gain2


def _log_sigmoid(x):
    return jnp.minimum(x, 0.0) - jnp.log(1.0 + jnp.exp(-jnp.abs(x)))


def _hgrn_chunk(q, fa, ia, ga, lb, gn, st_ref, hd, causal):
    chunk = q.shape[0]
    f = lb + (1.0 - lb) * jax.nn.sigmoid(fa)
    k = 1.0 - f
    b = _cumsum_rows(jnp.log(f))
    b_ref = b[chunk // 2:chunk // 2 + 1, :]
    b_last = b[chunk - 1:chunk, :]
    v = ia.astype(BF16)
    st = st_ref[hd]
    inter = lax.dot_general((q * jnp.exp(b)).astype(BF16), st.astype(BF16), _NT, preferred_element_type=F32)
    qr = (q * jnp.exp(b - b_ref)).astype(BF16)
    kr = (k * jnp.exp(b_ref - b)).astype(BF16)
    a = lax.dot_general(qr, kr, _NT, preferred_element_type=F32)
    a = jnp.where(causal, a, 0.0).astype(BF16)
    o = inter + jnp.dot(a, v, preferred_element_type=F32)
    k3 = (k * jnp.exp(b_last - b)).astype(BF16)
    st_ref[hd] = st * jnp.exp(b_last) + lax.dot_general(v, k3, _TN, preferred_element_type=F32)
    ms = jnp.mean(o * o, axis=-1, keepdims=True)
    return (o * lax.rsqrt(ms + EPS) * gn * (ga * jax.nn.sigmoid(ga))).astype(BF16)


def _mixer_in_kernel(x_ref, n1_ref, wf_ref, wh_ref, bf_ref, qg_ref, kg_ref, lbl_ref, gn_ref, s0_ref,
                     k_out, v_out, lf_out, qb_out, kb_out, vb_out, ccol_out, crow_out, cends_out,
                     oa_out, s_out,
                     h_scr, zf_scr, zh_even, zh_odd, c_scr, lf_scr, carry_scr, st_scr,
                     *, layer, aw, bw, nh, dh, nha, rb, tiles_per_seq, n_tiles):
    t = pl.program_id(0)
    ip = jnp.minimum(t, n_tiles - 1) % tiles_per_seq
    ih = jnp.maximum(t - 1, 0) % tiles_per_seq

    @pl.when(t == 0)
    def _():
        zh_odd[...] = jnp.zeros(zh_odd.shape, F32)
        st_scr[...] = jnp.zeros(st_scr.shape, F32)

    @pl.when(jnp.logical_and(ip == 0, t < n_tiles))
    def _():
        carry_scr[...] = jnp.zeros(carry_scr.shape, F32)

    @pl.when(jnp.logical_and(ih == 0, t >= 1))
    def _():
        for hd in range(nha):
            st_scr[hd] = s0_ref[0, hd].T

    x = x_ref[0]
    ta = x.shape[0]
    nrb = ta // rb
    nsl = wh_ref.shape[0]
    ms = jnp.mean(x * x, axis=-1, keepdims=True)
    h_scr[...] = (x * lax.rsqrt(ms + EPS) * n1_ref[...]).astype(BF16)
    zf_scr[...] = jnp.dot(h_scr[...], wf_ref[...], preferred_element_type=F32)

    interleave = nrb == nsl

    lg = lbl_ref[...]
    e = jnp.exp(lg - jnp.max(lg, axis=0, keepdims=True))
    lb = jnp.sum(e[:layer + 1], axis=0, keepdims=True) / jnp.sum(e, axis=0, keepdims=True)

    scale = dh ** -0.5 * LOG2E
    rb_idx = lax.broadcasted_iota(jnp.int32, (nrb, LANES), 0)
    causal = (lax.broadcasted_iota(jnp.int32, (rb, rb), 0) >= lax.broadcasted_iota(jnp.int32, (rb, rb), 1))
    spw = aw // MXU_COLS
    hps = MXU_COLS // LANES

    def trip(zh_new, zh_old, r, state):
        carry, cends = state
        rows = pl.ds(pl.multiple_of(r * rb, rb), rb)
        if interleave:
            zh_new[r] = jnp.dot(h_scr[...], wh_ref[r], preferred_element_type=F32)

        lf = _log_sigmoid(zf_scr[rows, 3 * bw:3 * bw + LANES] + bf_ref[...])
        lf_scr[rows, :] = lf
        c = _cumsum_rows(lf) + carry
        c2 = c * LOG2E
        c_scr[rows, :] = c2
        ccol_out[0, rows, :] = c2
        for p in range(nh // 2):
            cols = slice(LANES * p, LANES * (p + 1))
            qn = _pair_rms_norm(zf_scr[rows, LANES * p:LANES * (p + 1)], qg_ref[...], dh) * scale
            kn = _pair_rms_norm(zf_scr[rows, bw + LANES * p:bw + LANES * (p + 1)], kg_ref[...], dh)
            vv = zf_scr[rows, 2 * bw + LANES * p:2 * bw + LANES * (p + 1)]
            qb_out[0, rows, cols] = qn.astype(BF16)
            kb_out[0, rows, cols] = kn.astype(BF16)
            vb_out[0, rows, cols] = vv.astype(BF16)
            k_out[0, rows, cols] = kn
            v_out[0, rows, cols] = vv

        for hd in range(nha):
            sl, ln = hd // hps, slice(LANES * (hd % hps), LANES * (hd % hps + 1))
            cols = slice(LANES * hd, LANES * (hd + 1))
            oa_out[0, rows, cols] = _hgrn_chunk(
                zh_old[sl, rows, ln], zh_old[spw + sl, rows, ln],
                zh_old[2 * spw + sl, rows, ln], zh_old[3 * spw + sl, rows, ln],
                lb[:, cols], gn_ref[:, cols], st_scr, hd, causal)

        carry = c[rb - 1:rb, :]
        return carry, jnp.where(rb_idx == r, carry, cends)

    def run(zh_new, zh_old):
        if not interleave:
            for sl in range(nsl):
                zh_new[sl] = jnp.dot(h_scr[...], wh_ref[sl], preferred_element_type=F32)
        carry, cends = lax.fori_loop(0, nrb, functools.partial(trip, zh_new, zh_old),
                                     (carry_scr[...], jnp.zeros((nrb, LANES), F32)),
                                     unroll=max(u for u in (8, 4, 2, 1) if nrb % u == 0))
        cends_out[0] = cends

        @pl.when(t < n_tiles - 1)
        def _():
            carry_scr[...] = carry

    pl.when(t % 2 == 0)(lambda: run(zh_even, zh_odd))
    pl.when(t % 2 == 1)(lambda: run(zh_odd, zh_even))
    crow_out[0, 0] = c_scr[...].T[:nh, :]
    lf_out[0] = lf_scr[...].T[:nh, :]

    @pl.when(ih == tiles_per_seq - 1)
    def _():
        for hd in range(nha):
            s_out[0, hd] = st_scr[hd].T


def _mixer_in(x, n1, w_fox, w_hgrn, bf_pad, qg2, kg2, lb_logits, gnorm, s0, *, layer, nh, dh, chunk):
    b, t, d = x.shape
    nha, dk, dv = s0.shape[1], s0.shape[2], s0.shape[3]
    aw, bw = nha * dk, nh * dh
    ta = min(TOKEN_TILE, t)
    rb = chunk
    tps = t // ta
    n_tiles = b * tps
    assert ta % rb == 0 and t % ta == 0

    def proj_side(*tail):
        return lambda s: (jnp.minimum(s, n_tiles - 1) // tps, jnp.minimum(s, n_tiles - 1) % tps) + tail

    def rec_side(*tail):
        return lambda s: (jnp.maximum(s - 1, 0) // tps, jnp.maximum(s - 1, 0) % tps) + tail

    rec_seq = lambda s: (jnp.maximum(s - 1, 0) // tps, 0, 0, 0)
    tok = proj_side(0)
    out_shape = (
        jax.ShapeDtypeStruct((b, t, bw), F32),
        jax.ShapeDtypeStruct((b, t, bw), F32),
        jax.ShapeDtypeStruct((b, nh, t), F32),
        jax.ShapeDtypeStruct((b, t, bw), BF16),
        jax.ShapeDtypeStruct((b, t, bw), BF16),
        jax.ShapeDtypeStruct((b, t, bw), BF16),
        jax.ShapeDtypeStruct((b, t, LANES), F32),
        jax.ShapeDtypeStruct((b, tps, nh, ta), F32),
        jax.ShapeDtypeStruct((b, t // rb, LANES), F32),
        jax.ShapeDtypeStruct((b, t, aw), BF16),
        jax.ShapeDtypeStruct((b, nha, dk, dv), F32),
    )
    out_specs = (
        pl.BlockSpec((1, ta, bw), tok), pl.BlockSpec((1, ta, bw), tok),
        pl.BlockSpec((1, nh, ta), lambda s: (jnp.minimum(s, n_tiles - 1) // tps, 0, jnp.minimum(s, n_tiles - 1) % tps)),
        pl.BlockSpec((1, ta, bw), tok), pl.BlockSpec((1, ta, bw), tok), pl.BlockSpec((1, ta, bw), tok),
        pl.BlockSpec((1, ta, LANES), tok),
        pl.BlockSpec((1, 1, nh, ta), proj_side(0, 0)),
        pl.BlockSpec((1, ta // rb, LANES), tok),
        pl.BlockSpec((1, ta, aw), rec_side(0)),
        pl.BlockSpec((1, nha, dk, dv), rec_seq),
    )
    in_specs = [
        pl.BlockSpec((1, ta, d), tok),
        _const_spec(n1.shape), _const_spec(w_fox.shape), _const_spec(w_hgrn.shape), _const_spec(bf_pad.shape),
        _const_spec(qg2.shape), _const_spec(kg2.shape), _const_spec(lb_logits.shape), _const_spec(gnorm.shape),
        pl.BlockSpec((1, nha, dk, dv), rec_seq),
    ]
    nsl = w_hgrn.shape[0]
    return pl.pallas_call(
        functools.partial(_mixer_in_kernel, layer=layer, aw=aw, bw=bw, nh=nh, dh=dh, nha=nha, rb=rb,
                          tiles_per_seq=tps, n_tiles=n_tiles),
        out_shape=out_shape,
        grid=(n_tiles + 1,),
        in_specs=in_specs,
        out_specs=out_specs,
        scratch_shapes=[pltpu.VMEM((ta, d), BF16), pltpu.VMEM((ta, w_fox.shape[1]), F32),
                        pltpu.VMEM((nsl, ta, MXU_COLS), F32), pltpu.VMEM((nsl, ta, MXU_COLS), F32),
                        pltpu.VMEM((ta, LANES), F32), pltpu.VMEM((ta, LANES), F32),
                        pltpu.VMEM((1, LANES), F32), pltpu.VMEM((nha, dv, dk), F32)],
        compiler_params=pltpu.CompilerParams(dimension_semantics=("arbitrary",), vmem_limit_bytes=VMEM_LIMIT),
        name="mixer_in",
    )(x, n1, w_fox, w_hgrn, bf_pad, qg2, kg2, lb_logits, gnorm, s0)


def _one_head(q_pair, odd, dh):
    lane = lax.broadcasted_iota(jnp.int32, q_pair.shape, 1)
    keep = lane >= dh if odd else lane < dh
    return jnp.where(keep, q_pair, jnp.zeros_like(q_pair))


def _head_column(c_block, lane_idx):
    lane = lax.broadcasted_iota(jnp.int32, c_block.shape, 1)
    col = jnp.sum(jnp.where(lane == lane_idx, c_block, 0.0), axis=-1, keepdims=True)
    return jnp.broadcast_to(col, c_block.shape)


def _flash_step(q, k, v, ck, cq, m_ref, l_ref, acc_ref, hh, rows, mask):
    t = lax.dot_general(q, k, _NT, preferred_element_type=F32) - ck
    if mask is not None:
        t = jnp.where(mask, t, NEG_BIG)
    m_prev = m_ref[hh, rows, :]
    m_new = jnp.maximum(m_prev, jnp.max(t, axis=-1, keepdims=True) + cq)
    alpha = jnp.exp2(m_prev - m_new)
    off = m_new - cq
    ps = [jnp.exp2(t[:, LANES * c:LANES * (c + 1)] - off) for c in range(t.shape[1] // LANES)]
    l_ref[hh, rows, :] = alpha * l_ref[hh, rows, :] + sum(ps[1:], ps[0])
    p = jnp.concatenate(ps, axis=1).astype(BF16)
    acc_ref[hh, rows, :] = alpha * acc_ref[hh, rows, :] + jnp.dot(p, v, preferred_element_type=F32)
    m_ref[hh, rows, :] = m_new


def _fixed_offset_step(q, k, v, ck, cq, l_ref, acc_ref, hh, rows, mask):
    s = lax.dot_general(q, k, _NT, preferred_element_type=F32)
    ps = []
    for c in range(s.shape[1] // LANES):
        cols = slice(LANES * c, LANES * (c + 1))
        t = s[:, cols] - ck[:, cols] + cq
        if mask is not None:
            t = jnp.where(mask[:, cols], t, NEG_BIG)
        ps.append(jnp.exp2(t))
    l_ref[hh, rows, :] += sum(ps[1:], ps[0])
    p = jnp.concatenate(ps, axis=1).astype(BF16)
    acc_ref[hh, rows, :] += jnp.dot(p, v, preferred_element_type=F32)


def _merge_pair(acc0, l0, acc1, l1, dh):
    o0 = acc0 / jnp.sum(l0, axis=-1, keepdims=True)
    o1 = acc1 / jnp.sum(l1, axis=-1, keepdims=True)
    lane = lax.broadcasted_iota(jnp.int32, o0.shape, 1)
    return jnp.where(lane < dh, o0, o1)


def _attn_kernel(cb_ref, thr_ref, q_ref, k_ref, v_ref, crow_ref, ccol_ref, o_ref, m_scr, l_scr, acc_scr,
                 *, tq, dh, nh, hps, online):
    bi, g, i = pl.program_id(0), pl.program_id(1), pl.program_id(2)
    nblk = pl.num_programs(2)
    acc_scr[...] = jnp.zeros(acc_scr.shape, F32)
    l_scr[...] = jnp.zeros(l_scr.shape, F32)
    if online:
        m_scr[...] = jnp.full(m_scr.shape, NEG_BIG, F32)

    def step(q, k, v, ck, cq, hh, rows, mask):
        if online:
            _flash_step(q, k, v, ck, cq, m_scr, l_scr, acc_scr, hh, rows, mask)
        else:
            _fixed_offset_step(q, k, v, ck, cq, l_scr, acc_scr, hh, rows, mask)

    bases = [(bi * nh + hps * g + hh) * nblk for hh in range(hps)]
    c_before = [cb_ref[base + jnp.maximum(i - 1, 0)] for base in bases]

    def previous_is_live(n):
        live = False
        for base, cb_q in zip(bases, c_before):
            c_end = cb_ref[base + jnp.maximum(n - 1, 0)]
            live = jnp.logical_or(live, (cb_q - c_end) * LOG2E + thr_ref[0] >= EXP2_ZERO_BELOW)
        return jnp.logical_and(n > 0, live)

    j_lo = lax.while_loop(previous_is_live, lambda n: n - 1, i)

    def q_of(hh, rows=slice(None)):
        return _one_head(q_ref[0, rows, LANES * (hh // 2):LANES * (hh // 2 + 1)], hh % 2, dh)

    def cq_of(hh, rows=slice(None)):
        return _head_column(ccol_ref[0, rows, :], hps * g + hh)

    def body(j, _):
        keys = pl.ds(pl.multiple_of(j * tq, tq), tq)
        for hh in range(hps):
            kcols = slice(LANES * (hh // 2), LANES * (hh // 2 + 1))
            step(q_of(hh), k_ref[0, keys, kcols], v_ref[0, keys, kcols], crow_ref[0, j, 0, hh:hh + 1, :],
                 cq_of(hh), hh, slice(0, tq), None)
        return 0

    lax.fori_loop(j_lo, i, body, 0)

    half = tq // 2
    keys_a = pl.ds(pl.multiple_of(i * tq, tq), half)
    keys_b = pl.ds(pl.multiple_of(i * tq + half, half), half)
    mask_a = (lax.broadcasted_iota(jnp.int32, (tq, half), 0) >= lax.broadcasted_iota(jnp.int32, (tq, half), 1))
    mask_b = (lax.broadcasted_iota(jnp.int32, (half, half), 0) >= lax.broadcasted_iota(jnp.int32, (half, half), 1))
    late = slice(half, tq)
    for hh in range(hps):
        kcols = slice(LANES * (hh // 2), LANES * (hh // 2 + 1))
        step(q_of(hh), k_ref[0, keys_a, kcols], v_ref[0, keys_a, kcols], crow_ref[0, i, 0, hh:hh + 1, :half],
             cq_of(hh), hh, slice(0, tq), mask_a)
        step(q_of(hh, late), k_ref[0, keys_b, kcols], v_ref[0, keys_b, kcols],
             crow_ref[0, i, 0, hh:hh + 1, half:], cq_of(hh, late), hh, late, mask_b)
    for p in range(hps // 2):
        o_ref[0, :, LANES * p:LANES * (p + 1)] = _merge_pair(
            acc_scr[2 * p], l_scr[2 * p], acc_scr[2 * p + 1], l_scr[2 * p + 1], dh).astype(o_ref.dtype)


def _attn(cb, thr, qb, kb, vb, crow, ccol, *, nh, dh, online):
    b, t, _ = qb.shape
    tq = min(TOKEN_TILE, t)
    hps = min(ATTN_HEADS_PER_STEP, nh)
    ng = nh // hps
    crow5 = crow.reshape(b, t // tq, ng, hps, tq)
    grid_spec = pltpu.PrefetchScalarGridSpec(
        num_scalar_prefetch=2,
        grid=(b, ng, t // tq),
        in_specs=[pl.BlockSpec((1, tq, hps * dh), lambda bi, g, i, *_: (bi, i, g)),
                  pl.BlockSpec((1, t, hps * dh), lambda bi, g, i, *_: (bi, 0, g)),
                  pl.BlockSpec((1, t, hps * dh), lambda bi, g, i, *_: (bi, 0, g)),
                  pl.BlockSpec((1, t // tq, 1, hps, tq), lambda bi, g, i, *_: (bi, 0, g, 0, 0)),
                  pl.BlockSpec((1, tq, LANES), lambda bi, g, i, *_: (bi, i, 0))],
        out_specs=pl.BlockSpec((1, tq, hps * dh), lambda bi, g, i, *_: (bi, i, g)),
        scratch_shapes=[pltpu.VMEM((hps, tq, LANES), F32), pltpu.VMEM((hps, tq, LANES), F32),
                        pltpu.VMEM((hps, tq, LANES), F32)],
    )
    return pl.pallas_call(
        functools.partial(_attn_kernel, tq=tq, dh=dh, nh=nh, hps=hps, online=online),
        out_shape=jax.ShapeDtypeStruct((b, t, nh * dh), BF16),
        grid_spec=grid_spec,
        compiler_params=pltpu.CompilerParams(
            dimension_semantics=("arbitrary", "arbitrary", "arbitrary"), vmem_limit_bytes=VMEM_LIMIT),
        name="fox_attn_online" if online else "fox_attn",
    )(cb, thr, qb, kb, vb, crow5, ccol)


def _sample_attn_kernel(q_ref, kn_ref, vn_ref, crn_ref, ccol_ref, kc_ref, vc_ref, clf_ref, o_ref, c_scr,
                        *, nh, dh, rb):
    s_len, p_len = q_ref.shape[1], kc_ref.shape[3]

    def cache_rows(r, carry):
        rows = pl.ds(pl.multiple_of(r * rb, rb), rb)
        c = _cumsum_rows(clf_ref[0, rows, :]) + carry
        c_scr[rows, :] = c
        return c[rb - 1:rb, :]

    c_end = lax.fori_loop(0, p_len // rb, cache_rows, jnp.zeros((1, LANES), F32))
    c_cache = ((c_scr[...] - c_end) * LOG2E).T[:nh, :]

    causal = (lax.broadcasted_iota(jnp.int32, (s_len, s_len), 0)
              >= lax.broadcasted_iota(jnp.int32, (s_len, s_len), 1))
    for hd in range(nh):
        cols = slice(dh * hd, dh * (hd + 1))
        q = q_ref[0, :, cols]
        cq = ccol_ref[0, :, hd:hd + 1]
        kc_t = kc_ref[0, hd].astype(BF16)
        vc_t = vc_ref[0, hd].astype(BF16)
        t1 = jnp.dot(q, kc_t, preferred_element_type=F32) - c_cache[hd:hd + 1, :] + cq
        t2 = lax.dot_general(q, kn_ref[0, :, cols], _NT, preferred_element_type=F32) - crn_ref[0, 0, hd:hd + 1, :] + cq
        t2 = jnp.where(causal, t2, NEG_BIG)
        m = jnp.maximum(jnp.max(t1, axis=-1, keepdims=True), jnp.max(t2, axis=-1, keepdims=True))
        p1 = jnp.exp2(t1 - m)
        p2 = jnp.exp2(t2 - m)
        den = jnp.sum(p1, axis=-1, keepdims=True) + jnp.sum(p2, axis=-1, keepdims=True)
        acc = (lax.dot_general(p1.astype(BF16), vc_t, _NT, preferred_element_type=F32)
               + jnp.dot(p2.astype(BF16), vn_ref[0, :, cols], preferred_element_type=F32))
        o_ref[0, :, cols] = (acc / den).astype(o_ref.dtype)


def _sample_attn(qb, kb_n, vb_n, crow_n, ccol, cache_kt, cache_vt, clf_pad, *, nh, dh):
    b, s_len, bw = qb.shape
    p_len = cache_kt.shape[3]
    whole = lambda bi: (bi, 0, 0)
    whole4 = lambda bi: (bi, 0, 0, 0)
    return pl.pallas_call(
        functools.partial(_sample_attn_kernel, nh=nh, dh=dh, rb=min(HGRN_CHUNK, p_len)),
        out_shape=jax.ShapeDtypeStruct((b, s_len, bw), BF16),
        grid=(b,),
        in_specs=[pl.BlockSpec((1, s_len, bw), whole), pl.BlockSpec((1, s_len, bw), whole),
                  pl.BlockSpec((1, s_len, bw), whole), pl.BlockSpec((1, 1, nh, s_len), whole4),
                  pl.BlockSpec((1, s_len, LANES), whole),
                  pl.BlockSpec((1, nh, dh, p_len), whole4), pl.BlockSpec((1, nh, dh, p_len), whole4),
                  pl.BlockSpec((1, p_len, LANES), whole)],
        out_specs=pl.BlockSpec((1, s_len, bw), whole),
        scratch_shapes=[pltpu.VMEM((p_len, LANES), F32)],
        compiler_params=pltpu.CompilerParams(dimension_semantics=("arbitrary",), vmem_limit_bytes=VMEM_LIMIT),
        name="fox_sample_attn",
    )(qb, kb_n, vb_n, crow_n, ccol, cache_kt, cache_vt, clf_pad)


def _ffn_kernel(x_ref, oa_ref, ob_ref, woa_ref, wob_ref, n2_ref, wup_ref, wdn_ref, y_ref, h_scr, *, ff_chunk):
    x1 = (x_ref[...]
          + jnp.dot(oa_ref[...], woa_ref[...], preferred_element_type=F32)
          + jnp.dot(ob_ref[...], wob_ref[...], preferred_element_type=F32))
    ms = jnp.mean(x1 * x1, axis=-1, keepdims=True)
    h_scr[...] = (x1 * lax.rsqrt(ms + EPS) * n2_ref[...]).astype(BF16)
    y_ref[...] = x1
    for c in range(wup_ref.shape[1] // ff_chunk):
        cs = slice(ff_chunk * c, ff_chunk * (c + 1))
        u = jnp.maximum(jnp.dot(h_scr[...], wup_ref[:, cs], preferred_element_type=F32), 0.0)
        y_ref[...] += jnp.dot((u * u).astype(BF16), wdn_ref[cs, :], preferred_element_type=F32)


def _ffn(x2d, oa2d, ob2d, woa, wob, n2, wup, wdn):
    n, d = x2d.shape
    tc = min(TOKEN_TILE, n)
    aw, bw = oa2d.shape[1], ob2d.shape[1]
    row = lambda i: (i, 0)
    return pl.pallas_call(
        functools.partial(_ffn_kernel, ff_chunk=min(1024, wup.shape[1])),
        out_shape=jax.ShapeDtypeStruct((n, d), F32),
        grid=(n // tc,),
        in_specs=[pl.BlockSpec((tc, d), row), pl.BlockSpec((tc, aw), row), pl.BlockSpec((tc, bw), row),
                  _const_spec(woa.shape), _const_spec(wob.shape), _const_spec(n2.shape),
                  _const_spec(wup.shape), _const_spec(wdn.shape)],
        out_specs=pl.BlockSpec((tc, d), row),
        scratch_shapes=[pltpu.VMEM((tc, d), BF16)],
        compiler_params=pltpu.CompilerParams(
            dimension_semantics=("arbitrary",), vmem_limit_bytes=VMEM_LIMIT),
        name="out_ffn",
    )(x2d, oa2d, ob2d, woa, wob, n2, wup, wdn)


def kernel(x_prompt, x_sample, cache_fox_k, cache_fox_v, cache_fox_logf, state_hgrn, norm1, w_in, b_fox_f,
           q_norm_gain, k_norm_gain, hgrn_lb_logits, hgrn_out_norm, w_out, norm2, w_up, w_down):
    depth = w_in.shape[0]
    d = x_prompt.shape[-1]
    nh, dh = cache_fox_k.shape[3], cache_fox_k.shape[4]
    nha, dk, dv = state_hgrn.shape[2], state_hgrn.shape[3], state_hgrn.shape[4]
    aw, bw = nha * dk, nh * dh
    assert dk == LANES and dv == LANES and 2 * dh == LANES and nh % 2 == 0 and nh <= 8
    assert w_in.shape[2] == 4 * aw + 3 * bw + nh and aw + bw == w_out.shape[1] and aw % MXU_COLS == 0

    xp, xs = x_prompt, x_sample
    bp, tp, _ = xp.shape
    bs, ts, _ = xs.shape
    lb_logits = hgrn_lb_logits.astype(F32)

    kp, vp, lp, sp = [], [], [], []
    ksl, vsl, lsl, ssl = [], [], [], []
    for l in range(depth):
        w_fox = jnp.pad(w_in[l, :, 4 * aw:], ((0, 0), (0, LANES - nh))).astype(BF16)
        w_hgrn = jnp.transpose(w_in[l, :, :4 * aw].reshape(d, 4 * aw // MXU_COLS, MXU_COLS), (1, 0, 2)).astype(BF16)
        bf_pad = jnp.pad(b_fox_f[l].astype(F32), (0, LANES - nh)).reshape(1, LANES)
        qg2 = jnp.tile(q_norm_gain[l].astype(F32), 2).reshape(1, LANES)
        kg2 = jnp.tile(k_norm_gain[l].astype(F32), 2).reshape(1, LANES)
        n1 = norm1[l].astype(F32).reshape(1, d)
        n2 = norm2[l].astype(F32).reshape(1, d)
        gnorm = hgrn_out_norm[l].astype(F32).reshape(1, aw)
        woa = w_out[l, :aw].astype(BF16)
        wob = w_out[l, aw:].astype(BF16)
        wup = w_up[l].astype(BF16)
        wdn = w_down[l].astype(BF16)
        mixer_in = functools.partial(_mixer_in, layer=l, nh=nh, dh=dh)

        (k_p, v_p, lf_p, qb, kb, vb, ccol, crow, cends, oa, s_p) = mixer_in(
            xp, n1, w_fox, w_hgrn, bf_pad, qg2, kg2, lb_logits, gnorm, jnp.zeros((bp, nha, dk, dv), F32),
            chunk=min(HGRN_CHUNK, tp))
        per_blk = cends.shape[1] // (tp // min(TOKEN_TILE, tp))
        cb = jnp.transpose(cends[:, per_blk - 1::per_blk, :nh], (0, 2, 1)).reshape(-1)
        qk_bound = dh ** 0.5 * jnp.max(jnp.abs(q_norm_gain[l])) * jnp.max(jnp.abs(k_norm_gain[l]))
        thr = (2.04 * LOG2E * qk_bound + 2.0).astype(F32).reshape(1)
        ob = lax.cond(thr[0] <= 2.0 * FIXED_OFFSET_MAX_LOGIT,
                      functools.partial(_attn, nh=nh, dh=dh, online=False),
                      functools.partial(_attn, nh=nh, dh=dh, online=True),
                      cb, thr, qb, kb, vb, crow, ccol)
        xp = _ffn(xp.reshape(bp * tp, d), oa.reshape(bp * tp, aw), ob.reshape(bp * tp, bw),
                  woa, wob, n2, wup, wdn).reshape(bp, tp, d)
        kp.append(k_p.reshape(bp, tp, nh, dh))
        vp.append(v_p.reshape(bp, tp, nh, dh))
        lp.append(jnp.transpose(lf_p, (0, 2, 1)))
        sp.append(s_p.astype(x_prompt.dtype))

        clf_pad = jnp.pad(cache_fox_logf[l].astype(F32), ((0, 0), (0, 0), (0, LANES - nh)))
        (k_s, v_s, lf_s, qb, kb, vb, ccol, crow, _, oa, s_s) = mixer_in(
            xs, n1, w_fox, w_hgrn, bf_pad, qg2, kg2, lb_logits, gnorm, state_hgrn[l].astype(F32), chunk=ts)
        ob = _sample_attn(qb, kb, vb, crow, ccol, jnp.transpose(cache_fox_k[l].astype(F32), (0, 2, 3, 1)),
                          jnp.transpose(cache_fox_v[l].astype(F32), (0, 2, 3, 1)), clf_pad, nh=nh, dh=dh)
        xs = _ffn(xs.reshape(bs * ts, d), oa.reshape(bs * ts, aw), ob.reshape(bs * ts, bw),
                  woa, wob, n2, wup, wdn).reshape(bs, ts, d)
        ksl.append(k_s.reshape(bs, ts, nh, dh))
        vsl.append(v_s.reshape(bs, ts, nh, dh))
        lsl.append(jnp.transpose(lf_s, (0, 2, 1)))
        ssl.append(s_s.astype(state_hgrn.dtype))

    return (xp, xs, jnp.stack(kp), jnp.stack(vp), jnp.stack(lp), jnp.stack(sp),
            jnp.stack(ksl), jnp.stack(vsl), jnp.stack(lsl), jnp.stack(ssl))
```
